```python
import jax, jax.numpy as jnp
from jax import lax
import numpy as np

D_MODEL = 1024
BATCH = 8
SEQ = 2048
DEPTH = 4
DEC_BATCH = 32
DEC_SEQ = 1
PAST_LEN = 8192
PAGE_SIZE = 128

HEAD_DIM = 64
N_HEADS = D_MODEL // HEAD_DIM
FOX_HEADS = N_HEADS
DIL_CONFIGS = ((128, 1), (512, 4), (2048, 16))
N_DIL_GROUPS = len(DIL_CONFIGS)
DIL_HEADS = N_HEADS
DIL_BLOCK = 128
Q_BLOCK = 128
N_EXPERTS = 32
TOP_K = 4
D_FF = D_MODEL
MOE_BLOCK = 128
SWIGLU_ALPHA = 1.702
SWIGLU_LIMIT = 7.0
ROPE_THETA = 10000.0
LN_EPS = 1e-5
FORGET_BIAS_INIT = 8.0
N_MIXERS = 2
N_FOX_LAYERS = (DEPTH + 1) // 2
N_DIL_LAYERS = DEPTH // 2
DEEPNORM_ALPHA = (2 * DEPTH) ** 0.25
DEEPNORM_BETA = (8 * DEPTH) ** -0.25
SCALE = HEAD_DIM ** -0.5
NEG_INF = -1e30

kernel_name = 'fox_dilated_moe_hybrid_step'


def layer_norm(x, g, b):
    xf = x.astype(jnp.float32)
    mu = xf.mean(-1, keepdims=True)
    var = jnp.square(xf - mu).mean(-1, keepdims=True)
    return ((xf - mu) * lax.rsqrt(var + LN_EPS) * g.astype(jnp.float32) + b.astype(jnp.float32)).astype(x.dtype)


def rotary(x, pos):
    half = HEAD_DIM // 2
    inv = ROPE_THETA ** (-jnp.arange(half, dtype=jnp.float32) / half)
    ang = pos.astype(jnp.float32)[:, None] * inv[None, :]
    bshape = (ang.shape[0],) + (1,) * (x.ndim - 3) + (half,)
    cos, sin = jnp.cos(ang).reshape(bshape), jnp.sin(ang).reshape(bshape)
    x1, x2 = x[..., :half].astype(jnp.float32), x[..., half:].astype(jnp.float32)
    return jnp.concatenate([x1 * cos - x2 * sin, x2 * cos + x1 * sin], axis=-1).astype(x.dtype)


def fox_project(x, w_qkv, w_f, b_f):
    bx, t, _ = x.shape
    qkv = (x @ w_qkv).reshape(bx, t, 3, FOX_HEADS, HEAD_DIM)
    logf = jax.nn.log_sigmoid((x @ w_f + b_f).astype(jnp.float32))
    return qkv[:, :, 0], qkv[:, :, 1], qkv[:, :, 2], logf


def fox_prompt(x, w_qkv, w_f, b_f, w_o):
    bx, s_len, _ = x.shape
    q, k, v, logf = fox_project(x, w_qkv, w_f, b_f)
    c_t = jnp.cumsum(logf, axis=1).transpose(0, 2, 1)
    nb = s_len // Q_BLOCK
    q_blocks = q.reshape(bx, nb, Q_BLOCK, FOX_HEADS, HEAD_DIM).transpose(1, 0, 2, 3, 4)
    c_blocks = c_t.reshape(bx, FOX_HEADS, nb, Q_BLOCK).transpose(2, 0, 1, 3)
    kpos = jnp.arange(s_len)

    def one_block(args):
        qi, ci, bi = args
        s = jnp.einsum('bqhd,bkhd->bhqk', qi, k).astype(jnp.float32) * SCALE
        s = s + ci[..., :, None] - c_t[..., None, :]
        qpos = bi * Q_BLOCK + jnp.arange(Q_BLOCK)
        s = jnp.where(kpos[None, :] <= qpos[:, None], s, NEG_INF)
        p = jax.nn.softmax(s, axis=-1)
        return jnp.einsum('bhqk,bkhd->bqhd', p.astype(v.dtype), v)

    o = lax.map(one_block, (q_blocks, c_blocks, jnp.arange(nb)))
    o = o.transpose(1, 0, 2, 3, 4).reshape(bx, s_len, FOX_HEADS * HEAD_DIM)
    return o @ w_o, k, v, logf


def fox_sample(x, cache_k, cache_v, cache_logf, layer, page_table, w_qkv, w_f, b_f, w_o):
    bd, t, _ = x.shape
    q, k, v, logf = fox_project(x, w_qkv, w_f, b_f)
    n_pages = page_table.shape[1]
    past_lf = cache_logf[layer, page_table].astype(jnp.float32).reshape(bd, n_pages * PAGE_SIZE, FOX_HEADS)
    incl = jnp.cumsum(past_lf[:, ::-1], axis=1)[:, ::-1]
    suffix = jnp.concatenate([incl[:, 1:], jnp.zeros_like(incl[:, :1])], axis=1)
    suffix_pages = suffix.reshape(bd, n_pages, PAGE_SIZE, FOX_HEADS).transpose(1, 0, 3, 2)
    cq = jnp.cumsum(logf, axis=1).transpose(0, 2, 1)

    def page_step(carry, xs):
        m, l, acc = carry
        pt, sfx = xs
        kp = cache_k[layer, pt]
        vp = cache_v[layer, pt]
        s = jnp.einsum('bthd,bkhd->bhtk', q, kp).astype(jnp.float32) * SCALE + cq[..., None] + sfx[:, :, None, :]
        m_new = jnp.maximum(m, s.max(-1))
        corr = jnp.exp(m - m_new)
        p = jnp.exp(s - m_new[..., None])
        acc = acc * corr[..., None] + jnp.einsum('bhtk,bkhd->bhtd', p, vp.astype(jnp.float32))
        return (m_new, l * corr + p.sum(-1), acc), None

    init = (jnp.full((bd, FOX_HEADS, t), NEG_INF, jnp.float32),
            jnp.zeros((bd, FOX_HEADS, t), jnp.float32),
            jnp.zeros((bd, FOX_HEADS, t, HEAD_DIM), jnp.float32))
    (m, l, acc), _ = lax.scan(page_step, init, (page_table.T, suffix_pages))
    s = jnp.einsum('bthd,bshd->bhts', q, k).astype(jnp.float32) * SCALE + cq[..., :, None] - cq[..., None, :]
    causal = jnp.arange(t)[None, :] <= jnp.arange(t)[:, None]
    s = jnp.where(causal, s, NEG_INF)
    m_f = jnp.maximum(m, s.max(-1))
    corr = jnp.exp(m - m_f)
    p = jnp.exp(s - m_f[..., None])
    l = l * corr + p.sum(-1)
    acc = acc * corr[..., None] + jnp.einsum('bhts,bshd->bhtd', p, v.astype(jnp.float32))
    o = (acc / l[..., None]).transpose(0, 2, 1, 3).reshape(bd, t, FOX_HEADS * HEAD_DIM).astype(x.dtype)
    return o @ w_o, k, v, logf


def dil_project(x, w_qkv, pos):
    bx, t, _ = x.shape
    qkv = (x @ w_qkv).reshape(bx, t, N_DIL_GROUPS, 3, DIL_HEADS, HEAD_DIM)
    return rotary(qkv[:, :, :, 0], pos), rotary(qkv[:, :, :, 1], pos), qkv[:, :, :, 2]


def dilated_group_prompt(q, k, v, dil, n_steps):
    bx, s_len, h, dh = q.shape
    lu = s_len // dil
    nb = -(-lu // DIL_BLOCK)
    lp = nb * DIL_BLOCK

    def to_blocks(a):
        a = a.reshape(bx, lu, dil, h, dh).transpose(0, 2, 1, 3, 4)
        a = jnp.pad(a, ((0, 0), (0, 0), (0, lp - lu), (0, 0), (0, 0)))
        return a.reshape(bx, dil, nb, DIL_BLOCK, h, dh)

    def with_prev(a):
        prev = jnp.pad(a[:, :, :-1], ((0, 0), (0, 0), (1, 0), (0, 0), (0, 0), (0, 0)))
        return jnp.concatenate([prev, a], axis=3)

    qb = to_blocks(q)
    ks, vs = with_prev(to_blocks(k)), with_prev(to_blocks(v))
    s = jnp.einsum('brnqhd,brnkhd->brnhqk', qb, ks).astype(jnp.float32) * SCALE
    i = jnp.arange(DIL_BLOCK)[:, None]
    j = jnp.arange(2 * DIL_BLOCK)[None, :]
    diff = DIL_BLOCK + i - j
    valid = (diff >= 0) & (diff <= n_steps) & ((jnp.arange(nb)[:, None, None] > 0) | (j >= DIL_BLOCK))
    s = jnp.where(valid[None, None, :, None], s, NEG_INF)
    m = s.max(-1, keepdims=True)
    p = jnp.exp(s - m)
    l = p.sum(-1)
    o = jnp.einsum('brnhqk,brnkhd->brnqhd', p, vs.astype(jnp.float32)) / l.transpose(0, 1, 2, 4, 3)[..., None]
    lse = (m[..., 0] + jnp.log(l)).transpose(0, 1, 2, 4, 3)

    def from_blocks(a):
        a = a.reshape((bx, dil, lp) + a.shape[4:])[:, :, :lu]
        a = jnp.moveaxis(a, 1, 2)
        return a.reshape((bx, s_len) + a.shape[3:])

    return from_blocks(o), from_blocks(lse)


def dilated_group_sample(q, k, v, buf, dil, n_steps):
    l_buf, t = buf.shape[1], q.shape[1]
    kv_all = jnp.concatenate([buf, jnp.stack([k, v], axis=2)], axis=1)
    idx = l_buf + jnp.arange(t)[:, None] - dil * jnp.arange(n_steps + 1)[None, :]
    valid = idx >= 0
    g = kv_all[:, jnp.maximum(idx, 0)]
    s = jnp.einsum('bthd,btkhd->bthk', q, g[:, :, :, 0]).astype(jnp.float32) * SCALE
    s = jnp.where(valid[None, :, None, :], s, NEG_INF)
    m = s.max(-1, keepdims=True)
    p = jnp.exp(s - m)
    l = p.sum(-1)
    o = jnp.einsum('bthk,btkhd->bthd', p, g[:, :, :, 1].astype(jnp.float32)) / l[..., None]
    return o, m[..., 0] + jnp.log(l)


def merge_groups(outs, lses, dtype):
    w = jax.nn.softmax(jnp.stack(lses), axis=0)
    o = jnp.einsum('gbth,gbthd->bthd', w, jnp.stack(outs))
    return o.reshape(o.shape[0], o.shape[1], DIL_HEADS * HEAD_DIM).astype(dtype)


def dil_prompt(x, w_qkv, w_o, pos):
    s_len = x.shape[1]
    q, k, v = dil_project(x, w_qkv, pos)
    outs, lses, rows = [], [], []
    for gi, (win, dil) in enumerate(DIL_CONFIGS):
        o, lse = dilated_group_prompt(q[:, :, gi], k[:, :, gi], v[:, :, gi], dil, win // dil)
        outs.append(o)
        lses.append(lse)
        keep = min(win, s_len)
        rows.append(jnp.stack([k[:, s_len - keep:, gi], v[:, s_len - keep:, gi]], axis=2))
    return merge_groups(outs, lses, x.dtype) @ w_o, rows


def dil_sample(x, bufs, w_qkv, w_o, pos):
    q, k, v = dil_project(x, w_qkv, pos)
    outs, lses, rows = [], [], []
    for gi, (win, dil) in enumerate(DIL_CONFIGS):
        o, lse = dilated_group_sample(q[:, :, gi], k[:, :, gi], v[:, :, gi], bufs[gi], dil, win // dil)
        outs.append(o)
        lses.append(lse)
        rows.append(jnp.stack([k[:, :, gi], v[:, :, gi]], axis=2))
    return merge_groups(outs, lses, x.dtype) @ w_o, rows


def clamped_swiglu(h):
    glu = jnp.minimum(h[..., :D_FF], SWIGLU_LIMIT)
    lin = jnp.clip(h[..., D_FF:], -SWIGLU_LIMIT, SWIGLU_LIMIT)
    return glu * jax.nn.sigmoid(SWIGLU_ALPHA * glu) * (lin + 1)


def moe(x, w_router, b_router, w1, b1, w2, b2):
    shp = x.shape
    xt = x.reshape(-1, D_MODEL)
    n = xt.shape[0]
    logits = (xt @ w_router + b_router).astype(jnp.float32)
    top_val, top_idx = lax.top_k(logits, TOP_K)
    gates = jax.nn.softmax(top_val, axis=-1)
    nk = n * TOP_K
    flat_e = top_idx.reshape(-1)
    order = jnp.argsort(flat_e)
    sorted_e = flat_e[order]
    counts = jnp.bincount(flat_e, length=N_EXPERTS)
    padded = (counts + MOE_BLOCK - 1) // MOE_BLOCK * MOE_BLOCK
    pad_end = jnp.cumsum(padded)
    pad_start = pad_end - padded
    start = jnp.cumsum(counts) - counts
    dest = pad_start[sorted_e] + jnp.arange(nk) - start[sorted_e]
    n_groups = -(-nk // MOE_BLOCK) + N_EXPERTS
    row_token = jnp.full((n_groups * MOE_BLOCK,), n, jnp.int32).at[dest].set((order // TOP_K).astype(jnp.int32))
    xpad = jnp.concatenate([xt, jnp.zeros((1, D_MODEL), xt.dtype)], axis=0)
    xg = xpad[row_token].reshape(n_groups, MOE_BLOCK, D_MODEL)
    group_e = jnp.minimum(jnp.searchsorted(pad_end, jnp.arange(n_groups) * MOE_BLOCK, side='right'), N_EXPERTS - 1)

    def expert_group(args):
        xe, e = args
        h = xe @ w1[e] + b1[e]
        return clamped_swiglu(h) @ w2[e] + b2[e]

    yg = lax.map(expert_group, (xg, group_e)).reshape(-1, D_MODEL)
    y_assign = jnp.zeros((nk, D_MODEL), yg.dtype).at[order].set(yg[dest]).reshape(n, TOP_K, D_MODEL)
    y = jnp.einsum('nk,nkd->nd', gates.astype(y_assign.dtype), y_assign)
    return y.reshape(shp)


def setup_inputs(seed: int = 0) -> dict:
    key = jax.random.key(seed)
    keys = iter(jax.random.split(key, 32))

    def nrm(shape, scale=1.0):
        return jax.random.normal(next(keys), shape, jnp.float32) * scale

    n_pages = PAST_LEN // PAGE_SIZE
    n_used = DEC_BATCH * n_pages
    n_phys = n_used + (n_used + 3) // 4
    d_head_all = DIL_HEADS * HEAD_DIM
    fox_col = jnp.concatenate([jnp.ones((2 * D_MODEL,), jnp.float32),
                               jnp.full((D_MODEL,), DEEPNORM_BETA, jnp.float32)])
    dil_col = jnp.tile(jnp.repeat(jnp.array([1.0, 1.0, DEEPNORM_BETA], jnp.float32), d_head_all), N_DIL_GROUPS)
    inp = {}
    inp['x_prompt'] = nrm((BATCH, SEQ, D_MODEL))
    inp['x_sample'] = nrm((DEC_BATCH, DEC_SEQ, D_MODEL))
    inp['cache_fox_k'] = nrm((N_FOX_LAYERS, n_phys, PAGE_SIZE, FOX_HEADS, HEAD_DIM))
    inp['cache_fox_v'] = nrm((N_FOX_LAYERS, n_phys, PAGE_SIZE, FOX_HEADS, HEAD_DIM))
    inp['cache_fox_logf'] = jax.nn.log_sigmoid(FORGET_BIAS_INIT + nrm((N_FOX_LAYERS, n_phys, PAGE_SIZE, FOX_HEADS), 0.5))
    for win, _ in DIL_CONFIGS:
        inp['cache_dil_w%d' % win] = nrm((N_DIL_LAYERS, DEC_BATCH, min(win, PAST_LEN), 2, DIL_HEADS, HEAD_DIM))
    inp['page_table'] = jax.random.permutation(next(keys), n_phys)[:n_used].reshape(DEC_BATCH, n_pages).astype(jnp.int32)
    inp['fox_w_qkv'] = nrm((N_FOX_LAYERS, D_MODEL, 3 * D_MODEL), D_MODEL ** -0.5) * fox_col
    inp['fox_w_f'] = nrm((N_FOX_LAYERS, D_MODEL, FOX_HEADS), D_MODEL ** -0.5)
    inp['fox_b_f'] = FORGET_BIAS_INIT + nrm((N_FOX_LAYERS, FOX_HEADS), 0.5)
    inp['fox_w_o'] = nrm((N_FOX_LAYERS, D_MODEL, D_MODEL), D_MODEL ** -0.5 * DEEPNORM_BETA)
    inp['dil_w_qkv'] = nrm((N_DIL_LAYERS, D_MODEL, N_DIL_GROUPS * 3 * d_head_all), D_MODEL ** -0.5) * dil_col
    inp['dil_w_o'] = nrm((N_DIL_LAYERS, d_head_all, D_MODEL), d_head_all ** -0.5 * DEEPNORM_BETA)
    inp['moe_w_router'] = nrm((DEPTH, D_MODEL, N_EXPERTS), D_MODEL ** -0.5)
    inp['moe_b_router'] = nrm((DEPTH, N_EXPERTS), 0.01)
    inp['moe_w1'] = nrm((DEPTH, N_EXPERTS, D_MODEL, 2 * D_FF), D_MODEL ** -0.5 * DEEPNORM_BETA)
    inp['moe_b1'] = nrm((DEPTH, N_EXPERTS, 2 * D_FF), 0.01)
    inp['moe_w2'] = nrm((DEPTH, N_EXPERTS, D_FF, D_MODEL), D_FF ** -0.5 * DEEPNORM_BETA)
    inp['moe_b2'] = nrm((DEPTH, N_EXPERTS, D_MODEL), 0.01)
    inp['ln_g'] = 1.0 + nrm((DEPTH, 2, D_MODEL), 0.02)
    inp['ln_b'] = nrm((DEPTH, 2, D_MODEL), 0.02)
    return inp


def reference(x_prompt, x_sample, cache_fox_k, cache_fox_v, cache_fox_logf,
              cache_dil_w128, cache_dil_w512, cache_dil_w2048, page_table,
              fox_w_qkv, fox_w_f, fox_b_f, fox_w_o, dil_w_qkv, dil_w_o,
              moe_w_router, moe_b_router, moe_w1, moe_b1, moe_w2, moe_b2, ln_g, ln_b):
    dil_caches = (cache_dil_w128, cache_dil_w512, cache_dil_w2048)
    s_len, t_len = x_prompt.shape[1], x_sample.shape[1]
    past = page_table.shape[1] * PAGE_SIZE
    pos_prompt = jnp.arange(s_len)
    pos_sample = past + jnp.arange(t_len)
    xp, xs = x_prompt, x_sample
    fk_p, fk_s, fv_p, fv_s, fl_p, fl_s = [], [], [], [], [], []
    dil_p = [[] for _ in DIL_CONFIGS]
    dil_s = [[] for _ in DIL_CONFIGS]
    for i in range(DEPTH):
        j = i // N_MIXERS
        if i % N_MIXERS == 0:
            yp, k, v, lf = fox_prompt(xp, fox_w_qkv[j], fox_w_f[j], fox_b_f[j], fox_w_o[j])
            ys, k2, v2, lf2 = fox_sample(xs, cache_fox_k, cache_fox_v, cache_fox_logf, j, page_table,
                                         fox_w_qkv[j], fox_w_f[j], fox_b_f[j], fox_w_o[j])
            fk_p.append(k); fv_p.append(v); fl_p.append(lf)
            fk_s.append(k2); fv_s.append(v2); fl_s.append(lf2)
        else:
            yp, rows_p = dil_prompt(xp, dil_w_qkv[j], dil_w_o[j], pos_prompt)
            ys, rows_s = dil_sample(xs, [c[j] for c in dil_caches], dil_w_qkv[j], dil_w_o[j], pos_sample)
            for gi in range(N_DIL_GROUPS):
                dil_p[gi].append(rows_p[gi])
                dil_s[gi].append(rows_s[gi])
        xp = layer_norm(DEEPNORM_ALPHA * xp + yp, ln_g[i, 0], ln_b[i, 0])
        xs = layer_norm(DEEPNORM_ALPHA * xs + ys, ln_g[i, 0], ln_b[i, 0])
        xp = layer_norm(DEEPNORM_ALPHA * xp + moe(xp, moe_w_router[i], moe_b_router[i], moe_w1[i], moe_b1[i], moe_w2[i], moe_b2[i]),
                        ln_g[i, 1], ln_b[i, 1])
        xs = layer_norm(DEEPNORM_ALPHA * xs + moe(xs, moe_w_router[i], moe_b_router[i], moe_w1[i], moe_b1[i], moe_w2[i], moe_b2[i]),
                        ln_g[i, 1], ln_b[i, 1])
    return (xp, xs,
            jnp.stack(fk_p), jnp.stack(fk_s), jnp.stack(fv_p), jnp.stack(fv_s), jnp.stack(fl_p), jnp.stack(fl_s),
            jnp.stack(dil_p[0]), jnp.stack(dil_s[0]), jnp.stack(dil_p[1]), jnp.stack(dil_s[1]),
            jnp.stack(dil_p[2]), jnp.stack(dil_s[2]))
```

```python
import functools

import jax
import jax.numpy as jnp
from jax import lax
from jax.experimental import pallas as pl
from jax.experimental.pallas import tpu as pltpu

f32, bf16, i32 = jnp.float32, jnp.bfloat16, jnp.int32

D_MODEL = 1024
BATCH = 8
SEQ = 2048
DEPTH = 4
DEC_BATCH = 32
PAST_LEN = 8192
PAGE_SIZE = 128
HEAD_DIM = 64
N_HEADS = 16
DIL_CONFIGS = ((128, 1), (512, 4), (2048, 16))
N_EXPERTS = 32
TOP_K = 4
SWIGLU_ALPHA = 1.702
SWIGLU_LIMIT = 7.0
ROPE_THETA = 10000.0
LN_EPS = 1e-5
DEEPNORM_ALPHA = (2 * DEPTH) ** 0.25
SCALE = HEAD_DIM ** -0.5
NEG_INF = -1e30

LANES = 128
N_PROMPT = BATCH * SEQ
ROW_TILE = 512
NT = N_PROMPT + ROW_TILE
N_ROW_TILES = NT // ROW_TILE
TILES_PER_SEQ = SEQ // ROW_TILE
MOE_M = 256
MOE_G = NT * TOP_K // MOE_M + N_EXPERTS
N_PAGES = PAST_LEN // PAGE_SIZE
PAGES_PER_STEP = 4
VMEM_LIMIT = 56 * 1024 * 1024

_HI = lax.Precision.HIGHEST


def _cp(sem, vmem=VMEM_LIMIT):
    return pltpu.CompilerParams(dimension_semantics=sem, vmem_limit_bytes=vmem)


def _dot(a, b, precision=None):
    return jnp.dot(a, b, preferred_element_type=f32, precision=precision)


def _dot_nt(a, b):
    return lax.dot_general(a, b, (((1,), (1,)), ((), ())), preferred_element_type=f32)


def _layer_norm(z, g, b):
    mu = jnp.mean(z, axis=-1, keepdims=True)
    zc = z - mu
    var = jnp.mean(zc * zc, axis=-1, keepdims=True)
    return zc * lax.rsqrt(var + LN_EPS) * g + b


def _log_sigmoid(z):
    return jnp.minimum(z, 0.0) - jnp.log1p(jnp.exp(-jnp.abs(z)))


def _fox_proj_kernel(x_ref, w_ref, wf_ref, bf_ref,
                     q_ref, ktb_ref, vb_ref, kt_ref, vt_ref, lft_ref, ct_ref, ccol_ref, carry_ref):
    i = pl.program_id(0)

    @pl.when(i % TILES_PER_SEQ == 0)
    def _():
        carry_ref[...] = jnp.zeros_like(carry_ref)

    x = x_ref[...]
    acc = _dot(x.astype(bf16), w_ref[...])
    q_ref[...] = (acc[:, :D_MODEL] * SCALE).astype(bf16)
    kt = acc[:, D_MODEL:2 * D_MODEL].T
    kt_ref[...] = kt
    ktb_ref[...] = kt.astype(bf16)
    v = acc[:, 2 * D_MODEL:]
    vt_ref[...] = v.T
    vb_ref[...] = v.astype(bf16)
    lf = _log_sigmoid(_dot(x, wf_ref[...], _HI) + bf_ref[...])
    tm = x.shape[0]
    r = lax.broadcasted_iota(i32, (tm, tm), 0)
    c = lax.broadcasted_iota(i32, (tm, tm), 1)
    cs = _dot((c <= r).astype(f32), lf, _HI) + carry_ref[...]
    carry_ref[...] = cs[tm - 1:tm, :]
    ccol_ref[...] = cs
    lft_ref[...] = lf.T[:N_HEADS, :]
    ct_ref[...] = cs.T[:N_HEADS, :]


def fox_proj_prompt(x, w, wf, bfv):
    tm = ROW_TILE
    n = N_PROMPT // tm
    seq_map = lambda i: (i // TILES_PER_SEQ, 0, i % TILES_PER_SEQ)
    row_map = lambda i: (i, 0)
    const = lambda i: (0, 0)
    return pl.pallas_call(
        _fox_proj_kernel,
        grid=(n,),
        in_specs=[pl.BlockSpec((tm, D_MODEL), row_map),
                  pl.BlockSpec((D_MODEL, 3 * D_MODEL), const),
                  pl.BlockSpec((D_MODEL, LANES), const),
                  pl.BlockSpec((1, LANES), const)],
        out_specs=[pl.BlockSpec((tm, D_MODEL), row_map),
                   pl.BlockSpec((None, D_MODEL, tm), seq_map),
                   pl.BlockSpec((tm, D_MODEL), row_map),
                   pl.BlockSpec((None, D_MODEL, tm), seq_map),
                   pl.BlockSpec((None, D_MODEL, tm), seq_map),
                   pl.BlockSpec((None, N_HEADS, tm), seq_map),
                   pl.BlockSpec((None, N_HEADS, tm), seq_map),
                   pl.BlockSpec((tm, LANES), row_map)],
        out_shape=[jax.ShapeDtypeStruct((N_PROMPT, D_MODEL), bf16),
                   jax.ShapeDtypeStruct((BATCH, D_MODEL, SEQ), bf16),
                   jax.ShapeDtypeStruct((N_PROMPT, D_MODEL), bf16),
                   jax.ShapeDtypeStruct((BATCH, D_MODEL, SEQ), f32),
                   jax.ShapeDtypeStruct((BATCH, D_MODEL, SEQ), f32),
                   jax.ShapeDtypeStruct((BATCH, N_HEADS, SEQ), f32),
                   jax.ShapeDtypeStruct((BATCH, N_HEADS, SEQ), f32),
                   jax.ShapeDtypeStruct((N_PROMPT, LANES), f32)],
        scratch_shapes=[pltpu.VMEM((1, LANES), f32)],
        compiler_params=_cp(("arbitrary",)),
        name="fox_proj_prompt",
    )(x, w, wf, bfv)


def _fox_proj_sample_kernel(x_ref, w_ref, wf_ref, bf_ref, q_ref, k_ref, v_ref, lf_ref):
    x = x_ref[...]
    acc = _dot(x.astype(bf16), w_ref[...])
    q_ref[...] = acc[:, :D_MODEL] * SCALE
    k_ref[...] = acc[:, D_MODEL:2 * D_MODEL]
    v_ref[...] = acc[:, 2 * D_MODEL:]
    lf_ref[...] = _log_sigmoid(_dot(x, wf_ref[...], _HI) + bf_ref[...])


def fox_proj_sample(x, w, wf, bfv):
    nb = DEC_BATCH
    const = lambda i: (0, 0)
    return pl.pallas_call(
        _fox_proj_sample_kernel,
        grid=(1,),
        in_specs=[pl.BlockSpec((nb, D_MODEL), lambda i: (N_PROMPT // nb, 0)),
                  pl.BlockSpec((D_MODEL, 3 * D_MODEL), const),
                  pl.BlockSpec((D_MODEL, LANES), const),
                  pl.BlockSpec((1, LANES), const)],
        out_specs=[pl.BlockSpec((nb, D_MODEL), const)] * 3 + [pl.BlockSpec((nb, LANES), const)],
        out_shape=[jax.ShapeDtypeStruct((nb, D_MODEL), f32)] * 3 + [jax.ShapeDtypeStruct((nb, LANES), f32)],
        compiler_params=_cp(("arbitrary",)),
        name="fox_proj_sample",
    )(x, w, wf, bfv)


def _fox_attn_kernel(q_ref, kt_ref, v_ref, ccol_ref, ct_ref, o_ref, m_ref, l_ref, acc_ref, *, bq, bk):
    qi = pl.program_id(1)
    ki = pl.program_id(2)

    @pl.when(ki == 0)
    def _():
        m_ref[...] = jnp.full_like(m_ref, NEG_INF)
        l_ref[...] = jnp.zeros_like(l_ref)
        acc_ref[...] = jnp.zeros_like(acc_ref)

    @pl.when(ki <= qi)
    def _():
        row = qi * bq + lax.broadcasted_iota(i32, (bq, bk), 0)
        col = ki * bk + lax.broadcasted_iota(i32, (bq, bk), 1)
        causal = col <= row
        lo = lax.broadcasted_iota(i32, (1, LANES), 1) < HEAD_DIM
        for hp in range(N_HEADS // 2):
            sl = slice(hp * LANES, (hp + 1) * LANES)
            q2 = q_ref[:, sl]
            kt2 = kt_ref[sl, :]
            v2 = v_ref[:, sl]
            pvs, alphas = [], []
            for par in range(2):
                h = 2 * hp + par
                msk = lo if par == 0 else jnp.logical_not(lo)
                s = _dot(jnp.where(msk, q2, jnp.zeros_like(q2)), kt2)
                s = s + ccol_ref[:, h:h + 1] - ct_ref[h:h + 1, :]
                s = jnp.where(causal, s, NEG_INF)
                m_prev = m_ref[:, h:h + 1]
                m_new = jnp.maximum(m_prev, jnp.max(s, axis=1, keepdims=True))
                alpha = jnp.exp(m_prev - m_new)
                p = jnp.exp(s - m_new)
                l_ref[:, h:h + 1] = alpha * l_ref[:, h:h + 1] + jnp.sum(p, axis=1, keepdims=True)
                m_ref[:, h:h + 1] = m_new
                pvs.append(_dot(p.astype(bf16), jnp.where(msk, v2, jnp.zeros_like(v2))))
                alphas.append(alpha)
            acc_ref[:, sl] = acc_ref[:, sl] * jnp.where(lo, alphas[0], alphas[1]) + pvs[0] + pvs[1]

    @pl.when(ki == qi)
    def _():
        lo = lax.broadcasted_iota(i32, (1, LANES), 1) < HEAD_DIM
        for hp in range(N_HEADS // 2):
            sl = slice(hp * LANES, (hp + 1) * LANES)
            linv = jnp.where(lo, 1.0 / l_ref[:, 2 * hp:2 * hp + 1], 1.0 / l_ref[:, 2 * hp + 1:2 * hp + 2])
            o_ref[:, sl] = (acc_ref[:, sl] * linv).astype(o_ref.dtype)


def fox_attn_prompt(q, kt, v, ccol, ct):
    bq = bk = ROW_TILE
    nq = SEQ // bq
    kmap = lambda b, qi, ki: (b, 0, jnp.minimum(ki, qi))
    return pl.pallas_call(
        functools.partial(_fox_attn_kernel, bq=bq, bk=bk),
        grid=(BATCH, nq, nq),
        in_specs=[pl.BlockSpec((bq, D_MODEL), lambda b, qi, ki: (b * nq + qi, 0)),
                  pl.BlockSpec((None, D_MODEL, bk), kmap),
                  pl.BlockSpec((bk, D_MODEL), lambda b, qi, ki: (b * nq + jnp.minimum(ki, qi), 0)),
                  pl.BlockSpec((bq, LANES), lambda b, qi, ki: (b * nq + qi, 0)),
                  pl.BlockSpec((None, N_HEADS, bk), kmap)],
        out_specs=pl.BlockSpec((bq, D_MODEL), lambda b, qi, ki: (b * nq + qi, 0)),
        out_shape=jax.ShapeDtypeStruct((N_PROMPT, D_MODEL), bf16),
        scratch_shapes=[pltpu.VMEM((bq, LANES), f32), pltpu.VMEM((bq, LANES), f32), pltpu.VMEM((bq, D_MODEL), f32)],
        compiler_params=_cp(("parallel", "arbitrary", "arbitrary")),
        name="fox_attn_prompt",
    )(q, kt, v, ccol, ct)


def _row_to_cols(row):
    return jnp.broadcast_to(row, (LANES, row.shape[1])).T


def _decode_chunk(kt, vt, qt_ref, bias, valid, m_ref, l_ref, acc_ref, base=0):
    for h in range(N_HEADS):
        rows = slice(base + h * HEAD_DIM, base + (h + 1) * HEAD_DIM)
        hrow = slice(base // HEAD_DIM + h, base // HEAD_DIM + h + 1)
        u = jnp.sum(kt[h * HEAD_DIM:(h + 1) * HEAD_DIM, :] * qt_ref[rows, :], axis=0, keepdims=True)
        if bias is not None:
            u = u - bias[h:h + 1, :]
        if valid is not None:
            u = jnp.where(valid, u, NEG_INF)
        m_prev = m_ref[hrow, :]
        m_new = jnp.maximum(m_prev, u)
        alpha = jnp.exp(m_prev - m_new)
        p = jnp.exp(u - m_new)
        l_ref[hrow, :] = alpha * l_ref[hrow, :] + p
        m_ref[hrow, :] = m_new
        acc_ref[rows, :] = acc_ref[rows, :] * alpha + vt[h * HEAD_DIM:(h + 1) * HEAD_DIM, :] * p


def _decode_finish(h, qt_ref, knt, vnt, shift, m_ref, l_ref, acc_ref, base=0):
    rows = slice(base + h * HEAD_DIM, base + (h + 1) * HEAD_DIM)
    hrow = slice(base // HEAD_DIM + h, base // HEAD_DIM + h + 1)
    m_vec = m_ref[hrow, :]
    m_past = jnp.max(m_vec, axis=1, keepdims=True)
    sc = jnp.exp(m_vec - m_past)
    l_past = jnp.sum(l_ref[hrow, :] * sc, axis=1, keepdims=True)
    o_past = jnp.sum(acc_ref[rows, :] * sc, axis=1, keepdims=True)
    s_new = jnp.sum(qt_ref[rows, :] * knt[h * HEAD_DIM:(h + 1) * HEAD_DIM, :], axis=0, keepdims=True)
    m_sh = m_past + shift
    m_f = jnp.maximum(m_sh, s_new)
    a = jnp.exp(m_sh - m_f)
    bn = jnp.exp(s_new - m_f)
    l_f = l_past * a + bn
    o = (o_past * a + bn * vnt[h * HEAD_DIM:(h + 1) * HEAD_DIM, :]) / l_f
    return o, m_f + jnp.log(l_f)


def _fox_decode_kernel(pt_ref, q_ref, kn_ref, vn_ref, lfn_ref, *rest):
    np_ = PAGES_PER_STEP
    k_refs, v_refs, lf_refs = rest[:np_], rest[np_:2 * np_], rest[2 * np_:3 * np_]
    o_ref, qt_ref, m_ref, l_ref, acc_ref, carry_ref, ocol_ref = rest[3 * np_:]
    s = pl.program_id(1)

    @pl.when(s == 0)
    def _():
        qt_ref[...] = _row_to_cols(q_ref[...])
        m_ref[...] = jnp.full_like(m_ref, NEG_INF)
        l_ref[...] = jnp.zeros_like(l_ref)
        acc_ref[...] = jnp.zeros_like(acc_ref)
        carry_ref[...] = jnp.zeros_like(carry_ref)

    r = lax.broadcasted_iota(i32, (LANES, LANES), 0)
    c = lax.broadcasted_iota(i32, (LANES, LANES), 1)
    tri = (r <= c).astype(f32)
    for t in range(np_):
        lf = lf_refs[t][...]
        pre = _dot(lf, tri, _HI) + carry_ref[...]
        carry_ref[...] = carry_ref[...] + jnp.sum(lf, axis=1, keepdims=True)
        _decode_chunk(k_refs[t][...], v_refs[t][...], qt_ref, pre, None, m_ref, l_ref, acc_ref)

    @pl.when(s == pl.num_programs(1) - 1)
    def _():
        knt = _row_to_cols(kn_ref[...])
        vnt = _row_to_cols(vn_ref[...])
        for h in range(N_HEADS):
            shift = carry_ref[h:h + 1, :] + lfn_ref[:, h:h + 1]
            o, _ = _decode_finish(h, qt_ref, knt, vnt, shift, m_ref, l_ref, acc_ref)
            ocol_ref[h * HEAD_DIM:(h + 1) * HEAD_DIM, :] = jnp.broadcast_to(o, (HEAD_DIM, LANES))
        o_ref[...] = ocol_ref[...].T[0:1, :]


def fox_decode(page_table, q, kn, vn, lfn, ck, cv, cl, layer):
    np_ = PAGES_PER_STEP
    row = lambda b, s, pt: (b, 0, 0)

    def page_map(t):
        return lambda b, s, pt: (layer, pt[b, s * np_ + t], 0, 0)

    kv_specs = [pl.BlockSpec((None, None, D_MODEL, PAGE_SIZE), page_map(t)) for t in range(np_)]
    lf_specs = [pl.BlockSpec((None, None, N_HEADS, PAGE_SIZE), page_map(t)) for t in range(np_)]
    grid_spec = pltpu.PrefetchScalarGridSpec(
        num_scalar_prefetch=1,
        grid=(DEC_BATCH, N_PAGES // np_),
        in_specs=[pl.BlockSpec((None, 1, D_MODEL), row)] * 3 + [pl.BlockSpec((None, 1, LANES), row)]
                 + kv_specs + kv_specs + lf_specs,
        out_specs=pl.BlockSpec((None, 1, D_MODEL), row),
        scratch_shapes=[pltpu.VMEM((D_MODEL, LANES), f32),
                        pltpu.VMEM((N_HEADS, LANES), f32), pltpu.VMEM((N_HEADS, LANES), f32),
                        pltpu.VMEM((D_MODEL, LANES), f32), pltpu.VMEM((N_HEADS, LANES), f32),
                        pltpu.VMEM((D_MODEL, LANES), f32)])
    return pl.pallas_call(
        _fox_decode_kernel,
        grid_spec=grid_spec,
        out_shape=jax.ShapeDtypeStruct((DEC_BATCH, 1, D_MODEL), f32),
        compiler_params=_cp(("parallel", "arbitrary")),
        name="fox_decode",
    )(page_table, q, kn, vn, lfn, *([ck] * np_), *([cv] * np_), *([cl] * np_))


def _post_attn_kernel(o_ref, ot_ref, x_ref, wo_ref, g_ref, b_ref, wr_ref, br_ref,
                      x1_ref, idx_ref, gate_ref, rank_ref, cnt_ref, carry_ref, *, n_main):
    i = pl.program_id(0)

    @pl.when(i == 0)
    def _():
        carry_ref[...] = jnp.zeros_like(carry_ref)

    o = jnp.where(i < n_main, o_ref[...], ot_ref[...])
    x1 = _layer_norm(DEEPNORM_ALPHA * x_ref[...] + _dot(o, wo_ref[...]), g_ref[...], b_ref[...])
    x1_ref[...] = x1

    tm = x1.shape[0]
    logits = _dot(x1, wr_ref[...], _HI) + br_ref[...]
    lane = lax.broadcasted_iota(i32, (tm, LANES), 1)
    lane_f = lane.astype(f32)
    lg = logits
    onehots, vals, idxs = [], [], []
    for _ in range(TOP_K):
        mx = jnp.max(lg, axis=1, keepdims=True)
        ix = jnp.min(jnp.where(lg == mx, lane_f, float(LANES)), axis=1, keepdims=True)
        oh = lane_f == ix
        lg = jnp.where(oh, -3e38, lg)
        onehots.append(oh)
        vals.append(mx)
        idxs.append(ix)
    es = [jnp.exp(v - vals[0]) for v in vals]
    den = es[0] + es[1] + es[2] + es[3]
    chosen = jnp.zeros((tm, LANES), f32)
    for oh in onehots:
        chosen = chosen + oh.astype(f32)
    r = lax.broadcasted_iota(i32, (tm, tm), 0)
    c = lax.broadcasted_iota(i32, (tm, tm), 1)
    before = _dot((c < r).astype(bf16), chosen.astype(bf16)) + carry_ref[...]
    carry_ref[...] = carry_ref[...] + jnp.sum(chosen, axis=0, keepdims=True)
    cnt_ref[...] = carry_ref[...]
    idx_o = jnp.zeros((tm, LANES), f32)
    gate_o = jnp.zeros((tm, LANES), f32)
    rank_o = jnp.zeros((tm, LANES), f32)
    for k in range(TOP_K):
        sel = lane == k
        rk = jnp.sum(jnp.where(onehots[k], before, 0.0), axis=1, keepdims=True)
        idx_o = jnp.where(sel, idxs[k], idx_o)
        gate_o = jnp.where(sel, es[k] / den, gate_o)
        rank_o = jnp.where(sel, rk, rank_o)
    idx_ref[...] = idx_o.astype(i32)
    gate_ref[...] = gate_o
    rank_ref[...] = rank_o.astype(i32)


def post_attn(o_main, o_tail, x, wo, g, b, wr, br):
    tm = ROW_TILE
    n_main = N_PROMPT // tm
    row = lambda i: (i, 0)
    const = lambda i: (0, 0)
    return pl.pallas_call(
        functools.partial(_post_attn_kernel, n_main=n_main),
        grid=(N_ROW_TILES,),
        in_specs=[pl.BlockSpec((tm, D_MODEL), lambda i: (jnp.minimum(i, n_main - 1), 0)),
                  pl.BlockSpec((tm, D_MODEL), const),
                  pl.BlockSpec((tm, D_MODEL), row),
                  pl.BlockSpec((D_MODEL, D_MODEL), const),
                  pl.BlockSpec((1, D_MODEL), const),
                  pl.BlockSpec((1, D_MODEL), const),
                  pl.BlockSpec((D_MODEL, LANES), const),
                  pl.BlockSpec((1, LANES), const)],
        out_specs=[pl.BlockSpec((tm, D_MODEL), row),
                   pl.BlockSpec((tm, LANES), row),
                   pl.BlockSpec((tm, LANES), row),
                   pl.BlockSpec((tm, LANES), row),
                   pl.BlockSpec((1, LANES), const)],
        out_shape=[jax.ShapeDtypeStruct((NT, D_MODEL), f32),
                   jax.ShapeDtypeStruct((NT, LANES), i32),
                   jax.ShapeDtypeStruct((NT, LANES), f32),
                   jax.ShapeDtypeStruct((NT, LANES), i32),
                   jax.ShapeDtypeStruct((1, LANES), f32)],
        scratch_shapes=[pltpu.VMEM((1, LANES), f32)],
        compiler_params=_cp(("arbitrary",)),
        name="post_attn",
    )(o_main, o_tail, x, wo, g, b, wr, br)


def _moe_kernel(ge_ref, nv_ref, nu_ref, x_hbm, idx_hbm, w1_ref, b1_ref, w2_ref, b2_ref, y_hbm,
                xbuf, ybuf, idx_smem, w1b, w2b, gsem, ssem, isem):
    g = pl.program_id(0)
    nu = nu_ref[0]
    slot = g % 2
    other = 1 - slot
    m = MOE_M

    def idx_copy(gg, sl):
        return pltpu.make_async_copy(idx_hbm.at[gg], idx_smem.at[sl], isem.at[sl])

    def gather_row(sl, j):
        tok = idx_smem[sl, 0, j]
        return pltpu.make_async_copy(x_hbm.at[pl.ds(tok, 1)], xbuf.at[sl, pl.ds(j, 1)], gsem.at[sl])

    def scatter_row(sl, j):
        dst = idx_smem[sl, 1, j]
        return pltpu.make_async_copy(ybuf.at[sl, pl.ds(j, 1)], y_hbm.at[pl.ds(dst, 1)], ssem.at[sl])

    def start_rows(row_copy, sl, n):
        @pl.when(n == m)
        def _():
            def body8(j8, carry):
                for u in range(8):
                    row_copy(sl, j8 * 8 + u).start()
                return carry
            lax.fori_loop(0, m // 8, body8, 0)

        @pl.when(n < m)
        def _():
            def body(j, carry):
                row_copy(sl, j).start()
                return carry
            lax.fori_loop(0, n, body, 0)

    def wait_rows(n, wait_static):
        @pl.when(n == m)
        def _():
            wait_static(m)

        @pl.when(n < m)
        def _():
            bit = m // 2
            while bit >= 8:
                @pl.when((n & bit) != 0)
                def _(bit=bit):
                    wait_static(bit)
                bit //= 2

            def body(j, carry):
                wait_static(1)
                return carry
            lax.fori_loop(0, n & 7, body, 0)

    def wait_gathers(sl, n):
        wait_rows(n, lambda k: pltpu.make_async_copy(
            x_hbm.at[pl.ds(0, k)], xbuf.at[sl, pl.ds(0, k)], gsem.at[sl]).wait())

    def wait_scatters(sl, n):
        wait_rows(n, lambda k: pltpu.make_async_copy(
            ybuf.at[sl, pl.ds(0, k)], y_hbm.at[pl.ds(0, k)], ssem.at[sl]).wait())

    @pl.when(g < nu)
    def _():
        @pl.when(g == 0)
        def _():
            xbuf[...] = jnp.zeros_like(xbuf)
            first = idx_copy(0, 0)
            first.start()
            first.wait()
            start_rows(gather_row, 0, nv_ref[0])

            @pl.when(nu > 1)
            def _():
                idx_copy(1, 1).start()

        @pl.when(g + 1 < nu)
        def _():
            idx_copy(g + 1, other).wait()
            start_rows(gather_row, other, nv_ref[g + 1])

        @pl.when(g >= 2)
        def _():
            wait_scatters(slot, nv_ref[jnp.maximum(g - 2, 0)])

        wait_gathers(slot, nv_ref[g])

        @pl.when(jnp.logical_or(g == 0, ge_ref[g] != ge_ref[jnp.maximum(g - 1, 0)]))
        def _():
            w1b[...] = w1_ref[...].astype(bf16)
            w2b[...] = w2_ref[...].astype(bf16)

        x = xbuf[slot].astype(bf16)
        h = _dot(x, w1b[...]) + b1_ref[...]
        glu = jnp.minimum(h[:, :D_MODEL], SWIGLU_LIMIT)
        lin = jnp.clip(h[:, D_MODEL:], -SWIGLU_LIMIT, SWIGLU_LIMIT)
        act = glu * (1.0 / (1.0 + jnp.exp(-SWIGLU_ALPHA * glu))) * (lin + 1.0)
        ybuf[slot] = _dot(act.astype(bf16), w2b[...]) + b2_ref[...]

        start_rows(scatter_row, slot, nv_ref[g])

        @pl.when(g + 2 < nu)
        def _():
            idx_copy(g + 2, slot).start()

        @pl.when(g == nu - 1)
        def _():
            wait_scatters(slot, nv_ref[g])

            @pl.when(g >= 1)
            def _():
                wait_scatters(other, nv_ref[jnp.maximum(g - 1, 0)])


def moe_experts(ge, nv, nu, x, idx, w1, b1, w2, b2):
    m = MOE_M
    grid_spec = pltpu.PrefetchScalarGridSpec(
        num_scalar_prefetch=3,
        grid=(MOE_G,),
        in_specs=[pl.BlockSpec(memory_space=pl.ANY),
                  pl.BlockSpec(memory_space=pl.ANY),
                  pl.BlockSpec((None, D_MODEL, 2 * D_MODEL), lambda g, ge, nv, nu: (ge[g], 0, 0)),
                  pl.BlockSpec((None, 1, 2 * D_MODEL), lambda g, ge, nv, nu: (ge[g], 0, 0)),
                  pl.BlockSpec((None, D_MODEL, D_MODEL), lambda g, ge, nv, nu: (ge[g], 0, 0)),
                  pl.BlockSpec((None, 1, D_MODEL), lambda g, ge, nv, nu: (ge[g], 0, 0))],
        out_specs=pl.BlockSpec(memory_space=pl.ANY),
        scratch_shapes=[pltpu.VMEM((2, m, D_MODEL), f32),
                        pltpu.VMEM((2, m, D_MODEL), f32),
                        pltpu.SMEM((2, 2, m), i32),
                        pltpu.VMEM((D_MODEL, 2 * D_MODEL), bf16),
                        pltpu.VMEM((D_MODEL, D_MODEL), bf16),
                        pltpu.SemaphoreType.DMA((2,)),
                        pltpu.SemaphoreType.DMA((2,)),
                        pltpu.SemaphoreType.DMA((2,))])
    return pl.pallas_call(
        _moe_kernel,
        grid_spec=grid_spec,
        out_shape=jax.ShapeDtypeStruct((TOP_K * NT, D_MODEL), f32),
        compiler_params=_cp(("arbitrary",)),
        name="moe_experts",
    )(ge, nv, nu, x, idx, w1, b1, w2, b2)


def _route(idx, rank, cnt):
    m, g = MOE_M, MOE_G
    idx4 = idx[:, :TOP_K]
    rank4 = rank[:, :TOP_K]
    counts = cnt[0, :N_EXPERTS].astype(i32)
    padded = (counts + m - 1) // m * m
    pad_end = jnp.cumsum(padded)
    pad_start = pad_end - padded
    dest = pad_start[idx4] + rank4
    assign = jnp.arange(NT, dtype=i32)[:, None] + NT * jnp.arange(TOP_K, dtype=i32)[None, :]
    rows = jnp.zeros((g * m,), i32).at[dest.reshape(-1)].set(assign.reshape(-1), unique_indices=True)
    idx_arr = jnp.stack([rows % NT, rows], axis=0).reshape(2, g, m).transpose(1, 0, 2)
    gstart = jnp.arange(g, dtype=i32) * m
    ge = jnp.minimum(jnp.searchsorted(pad_end, gstart, side='right'), N_EXPERTS - 1).astype(i32)
    nv = jnp.clip(counts[ge] - (gstart - pad_start[ge]), 0, m).astype(i32)
    nu = (pad_end[-1:] // m).astype(i32)
    return ge, nv, nu, idx_arr


def _moe_combine_kernel(y0, y1, y2, y3, gate_ref, x_ref, g_ref, b_ref, o_ref):
    gt = gate_ref[...]
    y = gt[:, 0:1] * y0[...] + gt[:, 1:2] * y1[...] + gt[:, 2:3] * y2[...] + gt[:, 3:4] * y3[...]
    o_ref[...] = _layer_norm(DEEPNORM_ALPHA * x_ref[...] + y, g_ref[...], b_ref[...])


def moe_combine(y, gate, x, g, b):
    tm = ROW_TILE
    row = lambda i: (i, 0)
    const = lambda i: (0, 0)
    y_specs = [pl.BlockSpec((tm, D_MODEL), (lambda i, k=k: (k * N_ROW_TILES + i, 0))) for k in range(TOP_K)]
    return pl.pallas_call(
        _moe_combine_kernel,
        grid=(N_ROW_TILES,),
        in_specs=y_specs + [pl.BlockSpec((tm, LANES), row), pl.BlockSpec((tm, D_MODEL), row),
                            pl.BlockSpec((1, D_MODEL), const), pl.BlockSpec((1, D_MODEL), const)],
        out_specs=pl.BlockSpec((tm, D_MODEL), row),
        out_shape=jax.ShapeDtypeStruct((NT, D_MODEL), f32),
        compiler_params=_cp(("parallel",)),
        name="moe_combine",
    )(y, y, y, y, gate, x, g, b)


def _rope(xb, cos, sin, lo32):
    partner = jnp.where(lo32, pltpu.roll(xb, LANES - HEAD_DIM // 2, 1), pltpu.roll(xb, HEAD_DIM // 2, 1))
    return xb * cos + partner * sin


def _rope_tables(pos):
    half = HEAD_DIM // 2
    lane = jnp.arange(LANES)
    inv = ROPE_THETA ** (-(lane % half).astype(f32) / half)
    ang = pos.astype(f32)[:, None] * inv[None, :]
    sign = jnp.where((lane % HEAD_DIM) < half, -1.0, 1.0).astype(f32)
    return jnp.cos(ang), jnp.sin(ang) * sign[None, :]


def _dil_proj_kernel(x_ref, w_ref, cos_ref, sin_ref, q_ref, k_ref, v_ref, kvt_ref, *scratch, dil):
    tm = x_ref.shape[0]
    nblk = D_MODEL // LANES
    acc = _dot(x_ref[...].astype(bf16), w_ref[...])
    cos, sin = cos_ref[...], sin_ref[...]
    lo32 = (lax.broadcasted_iota(i32, (1, LANES), 1) % HEAD_DIM) < HEAD_DIM // 2
    pieces = []
    for cb in range(3 * nblk):
        blk = acc[:, cb * LANES:(cb + 1) * LANES]
        if cb < nblk:
            blk = _rope(blk * SCALE, cos, sin, lo32)
        elif cb < 2 * nblk:
            blk = _rope(blk, cos, sin, lo32)
        pieces.append(blk)
    kvt_ref[0] = jnp.concatenate(pieces[nblk:2 * nblk], axis=1).T
    kvt_ref[1] = jnp.concatenate(pieces[2 * nblk:], axis=1).T
    outs = (q_ref, k_ref, v_ref)
    if dil == 1:
        for cb in range(3 * nblk):
            outs[cb // nblk][0, :, (cb % nblk) * LANES:(cb % nblk + 1) * LANES] = pieces[cb].astype(bf16)
    else:
        slab = scratch[0]
        for cb in range(3 * nblk):
            slab[cb] = pieces[cb]
        n = tm // dil
        for cb in range(3 * nblk):
            for r in range(dil):
                outs[cb // nblk][r, :, (cb % nblk) * LANES:(cb % nblk + 1) * LANES] = (
                    slab[cb, pl.ds(r, n, stride=dil), :].astype(bf16))


def dil_proj_prompt(x, w, cos, sin, group):
    dil = DIL_CONFIGS[group][1]
    tm = ROW_TILE
    lu = SEQ // dil
    n = N_PROMPT // tm
    res_map = lambda i: (i // TILES_PER_SEQ, 0, i % TILES_PER_SEQ, 0)
    res_spec = pl.BlockSpec((None, dil, tm // dil, D_MODEL), res_map)
    res_shape = jax.ShapeDtypeStruct((BATCH, dil, lu, D_MODEL), bf16)
    return pl.pallas_call(
        functools.partial(_dil_proj_kernel, dil=dil),
        grid=(n,),
        in_specs=[pl.BlockSpec((tm, D_MODEL), lambda i: (i, 0)),
                  pl.BlockSpec((D_MODEL, 3 * D_MODEL), lambda i: (0, group)),
                  pl.BlockSpec((tm, LANES), lambda i: (i % TILES_PER_SEQ, 0)),
                  pl.BlockSpec((tm, LANES), lambda i: (i % TILES_PER_SEQ, 0))],
        out_specs=[res_spec, res_spec, res_spec,
                   pl.BlockSpec((None, 2, D_MODEL, tm), lambda i: (i // TILES_PER_SEQ, 0, 0, i % TILES_PER_SEQ))],
        out_shape=[res_shape, res_shape, res_shape, jax.ShapeDtypeStruct((BATCH, 2, D_MODEL, SEQ), f32)],
        scratch_shapes=[] if dil == 1 else [pltpu.VMEM((3 * D_MODEL // LANES, tm, LANES), f32)],
        compiler_params=_cp(("parallel",)),
        name=f"dil_proj_prompt_g{group}",
    )(x, w, cos, sin)


def _dil_proj_sample_kernel(x_ref, w_ref, cos_ref, sin_ref, q_ref, k_ref, v_ref):
    nblk = D_MODEL // LANES
    acc = _dot(x_ref[...].astype(bf16), w_ref[...])
    cos, sin = cos_ref[...], sin_ref[...]
    lo32 = (lax.broadcasted_iota(i32, (1, LANES), 1) % HEAD_DIM) < HEAD_DIM // 2
    for gi in range(len(DIL_CONFIGS)):
        for cb in range(3 * nblk):
            col = gi * 3 * D_MODEL + cb * LANES
            blk = acc[:, col:col + LANES]
            dst = slice((cb % nblk) * LANES, (cb % nblk + 1) * LANES)
            if cb < nblk:
                q_ref[gi, :, dst] = _rope(blk * SCALE, cos, sin, lo32)
            elif cb < 2 * nblk:
                k_ref[gi, :, dst] = _rope(blk, cos, sin, lo32)
            else:
                v_ref[gi, :, dst] = blk


def dil_proj_sample(x, w, cos, sin):
    nb = DEC_BATCH
    ng = len(DIL_CONFIGS)
    const = lambda i: (0, 0)
    out_spec = pl.BlockSpec((ng, nb, D_MODEL), lambda i: (0, 0, 0))
    out_shape = jax.ShapeDtypeStruct((ng, nb, D_MODEL), f32)
    return pl.pallas_call(
        _dil_proj_sample_kernel,
        grid=(1,),
        in_specs=[pl.BlockSpec((nb, D_MODEL), lambda i: (N_PROMPT // nb, 0)),
                  pl.BlockSpec((D_MODEL, ng * 3 * D_MODEL), const),
                  pl.BlockSpec((1, LANES), const),
                  pl.BlockSpec((1, LANES), const)],
        out_specs=[out_spec] * 3,
        out_shape=[out_shape] * 3,
        compiler_params=_cp(("arbitrary",)),
        name="dil_proj_sample",
    )(x, w, cos, sin)


def _dil_attn_kernel(q_ref, kc_ref, kp_ref, vc_ref, vp_ref, o_ref, lse_ref):
    n = pl.program_id(2)
    blk = q_ref.shape[0]
    i = lax.broadcasted_iota(i32, (blk, blk), 0)
    j = lax.broadcasted_iota(i32, (blk, blk), 1)
    cur_ok = j <= i
    prev_ok = jnp.logical_and(j >= i, n > 0)
    lane = lax.broadcasted_iota(i32, (1, LANES), 1)
    lo = lane < HEAD_DIM
    lse = jnp.zeros((blk, LANES), f32)
    for hp in range(N_HEADS // 2):
        sl = slice(hp * LANES, (hp + 1) * LANES)
        q2, kc, kp, vc, vp = q_ref[:, sl], kc_ref[:, sl], kp_ref[:, sl], vc_ref[:, sl], vp_ref[:, sl]
        o2 = jnp.zeros((blk, LANES), f32)
        for par in range(2):
            msk = lo if par == 0 else jnp.logical_not(lo)
            qh = jnp.where(msk, q2, jnp.zeros_like(q2))
            sc = jnp.where(cur_ok, _dot_nt(qh, kc), NEG_INF)
            sp = jnp.where(prev_ok, _dot_nt(qh, kp), NEG_INF)
            mx = jnp.maximum(jnp.max(sc, axis=1, keepdims=True), jnp.max(sp, axis=1, keepdims=True))
            pc = jnp.exp(sc - mx)
            pp = jnp.exp(sp - mx)
            l = jnp.sum(pc, axis=1, keepdims=True) + jnp.sum(pp, axis=1, keepdims=True)
            pv = (_dot(pc.astype(bf16), jnp.where(msk, vc, jnp.zeros_like(vc)))
                  + _dot(pp.astype(bf16), jnp.where(msk, vp, jnp.zeros_like(vp))))
            o2 = o2 + pv * (1.0 / l)
            lse = jnp.where(lane == 2 * hp + par, mx + jnp.log(l), lse)
        o_ref[:, sl] = o2
    lse_ref[...] = lse


def dil_attn_prompt(q, k, v):
    _, dil, lu, _ = q.shape
    blk = 128
    nb = lu // blk
    cur = lambda b, r, n: (b, r, n, 0)
    prev = lambda b, r, n: (b, r, jnp.maximum(n - 1, 0), 0)
    spec = lambda m: pl.BlockSpec((None, None, blk, D_MODEL), m)
    return pl.pallas_call(
        _dil_attn_kernel,
        grid=(BATCH, dil, nb),
        in_specs=[spec(cur), spec(cur), spec(prev), spec(cur), spec(prev)],
        out_specs=[spec(cur), pl.BlockSpec((None, None, blk, LANES), cur)],
        out_shape=[jax.ShapeDtypeStruct((BATCH, dil, lu, D_MODEL), f32),
                   jax.ShapeDtypeStruct((BATCH, dil, lu, LANES), f32)],
        compiler_params=_cp(("parallel", "parallel", "arbitrary")),
        name=f"dil_attn_prompt_d{dil}",
    )(q, k, k, v, v)


def _expand_heads(w):
    hrow = lax.broadcasted_iota(i32, (LANES, D_MODEL), 0)
    col = lax.broadcasted_iota(i32, (LANES, D_MODEL), 1)
    e = (col // HEAD_DIM == hrow).astype(bf16)
    hi = w.astype(bf16)
    lo = (w - hi.astype(f32)).astype(bf16)
    return _dot(hi, e) + _dot(lo, e)


def _dil_merge_kernel(o1_ref, o2_ref, o3_ref, l1_ref, l2_ref, l3_ref, out_ref, nat_o, nat_l):
    tm = out_ref.shape[0]
    nblk = D_MODEL // LANES
    for gi, (o_ref, l_ref) in enumerate(((o2_ref, l2_ref), (o3_ref, l3_ref))):
        dil = DIL_CONFIGS[gi + 1][1]
        n = tm // dil
        for r in range(dil):
            nat_l[gi, pl.ds(r, n, stride=dil), :] = l_ref[r]
            for cb in range(nblk):
                nat_o[gi, cb, pl.ds(r, n, stride=dil), :] = o_ref[r, :, cb * LANES:(cb + 1) * LANES]
    lses = (l1_ref[0], nat_l[0], nat_l[1])
    mx = jnp.maximum(jnp.maximum(lses[0], lses[1]), lses[2])
    es = [jnp.exp(l - mx) for l in lses]
    den = es[0] + es[1] + es[2]
    ws = [_expand_heads(e / den) for e in es]
    for cb in range(nblk):
        sl = slice(cb * LANES, (cb + 1) * LANES)
        out_ref[:, sl] = (ws[0][:, sl] * o1_ref[0, :, sl] + ws[1][:, sl] * nat_o[0, cb]
                          + ws[2][:, sl] * nat_o[1, cb]).astype(out_ref.dtype)


def dil_merge_prompt(o1, o2, o3, l1, l2, l3):
    tm = ROW_TILE
    res_map = lambda b, t: (b, 0, t, 0)

    def ospec(dil, width):
        return pl.BlockSpec((None, dil, tm // dil, width), res_map)

    dils = [d for _, d in DIL_CONFIGS]
    return pl.pallas_call(
        _dil_merge_kernel,
        grid=(BATCH, TILES_PER_SEQ),
        in_specs=[ospec(d, D_MODEL) for d in dils] + [ospec(d, LANES) for d in dils],
        out_specs=pl.BlockSpec((tm, D_MODEL), lambda b, t: (b * TILES_PER_SEQ + t, 0)),
        out_shape=jax.ShapeDtypeStruct((N_PROMPT, D_MODEL), bf16),
        scratch_shapes=[pltpu.VMEM((2, D_MODEL // LANES, tm, LANES), f32), pltpu.VMEM((2, tm, LANES), f32)],
        compiler_params=_cp(("parallel", "parallel")),
        name="dil_merge_prompt",
    )(o1, o2, o3, l1, l2, l3)


def _dil_decode_kernel(q_ref, kn_ref, vn_ref, c1_ref, c2_ref, c3_ref, o_ref,
                       qt_ref, m_ref, l_ref, acc_ref, ocol_ref):
    c = pl.program_id(1)
    ng = len(DIL_CONFIGS)
    lane = lax.broadcasted_iota(i32, (1, LANES), 1)

    def group_chunks(gi, cache_ref, n_chunks, pos0):
        dil = DIL_CONFIGS[gi][1]
        for t in range(n_chunks):
            valid = None if dil == 1 else ((pos0 + t * LANES + lane) % dil) == 0
            _decode_chunk(cache_ref[0, :, t * LANES:(t + 1) * LANES], cache_ref[1, :, t * LANES:(t + 1) * LANES],
                          qt_ref, None, valid, m_ref, l_ref, acc_ref, base=gi * D_MODEL)

    @pl.when(c == 0)
    def _():
        for gi in range(ng):
            qt_ref[gi * D_MODEL:(gi + 1) * D_MODEL, :] = _row_to_cols(q_ref[gi:gi + 1, :])
        m_ref[...] = jnp.full_like(m_ref, NEG_INF)
        l_ref[...] = jnp.zeros_like(l_ref)
        acc_ref[...] = jnp.zeros_like(acc_ref)
        group_chunks(0, c1_ref, c1_ref.shape[2] // LANES, 0)
        group_chunks(1, c2_ref, c2_ref.shape[2] // LANES, 0)

    group_chunks(2, c3_ref, c3_ref.shape[2] // LANES, c * c3_ref.shape[2])

    @pl.when(c == pl.num_programs(1) - 1)
    def _():
        knts = [_row_to_cols(kn_ref[gi:gi + 1, :]) for gi in range(ng)]
        vnts = [_row_to_cols(vn_ref[gi:gi + 1, :]) for gi in range(ng)]
        for h in range(N_HEADS):
            outs = [_decode_finish(h, qt_ref, knts[gi], vnts[gi], 0.0, m_ref, l_ref, acc_ref, base=gi * D_MODEL)
                    for gi in range(ng)]
            mx = jnp.maximum(jnp.maximum(outs[0][1], outs[1][1]), outs[2][1])
            es = [jnp.exp(o[1] - mx) for o in outs]
            den = es[0] + es[1] + es[2]
            merged = (es[0] * outs[0][0] + es[1] * outs[1][0] + es[2] * outs[2][0]) / den
            ocol_ref[h * HEAD_DIM:(h + 1) * HEAD_DIM, :] = jnp.broadcast_to(merged, (HEAD_DIM, LANES))
        o_ref[...] = ocol_ref[...].T[0:1, :]


def dil_decode(q, kn, vn, c1, c2, c3, layer):
    ng = len(DIL_CONFIGS)
    chunk = 512
    n_chunks = c3.shape[-1] // chunk
    row = lambda b, c: (b, 0, 0)
    whole = lambda b, c: (layer, b, 0, 0, 0)
    return pl.pallas_call(
        _dil_decode_kernel,
        grid=(DEC_BATCH, n_chunks),
        in_specs=[pl.BlockSpec((None, ng, D_MODEL), row)] * 3
                 + [pl.BlockSpec((None, None, 2, D_MODEL, c1.shape[-1]), whole),
                    pl.BlockSpec((None, None, 2, D_MODEL, c2.shape[-1]), whole),
                    pl.BlockSpec((None, None, 2, D_MODEL, chunk), lambda b, c: (layer, b, 0, 0, c))],
        out_specs=pl.BlockSpec((None, 1, D_MODEL), row),
        out_shape=jax.ShapeDtypeStruct((DEC_BATCH, 1, D_MODEL), f32),
        scratch_shapes=[pltpu.VMEM((ng * D_MODEL, LANES), f32),
                        pltpu.VMEM((ng * N_HEADS, LANES), f32), pltpu.VMEM((ng * N_HEADS, LANES), f32),
                        pltpu.VMEM((ng * D_MODEL, LANES), f32), pltpu.VMEM((D_MODEL, LANES), f32)],
        compiler_params=_cp(("parallel", "arbitrary")),
        name="dil_decode",
    )(q, kn, vn, c1, c2, c3)


def kernel(x_prompt, x_sample, cache_fox_k, cache_fox_v, cache_fox_logf, cache_dil_w128, cache_dil_w512,
           cache_dil_w2048, page_table, fox_w_qkv, fox_w_f, fox_b_f, fox_w_o, dil_w_qkv, dil_w_o,
           moe_w_router, moe_b_router, moe_w1, moe_b1, moe_w2, moe_b2, ln_g, ln_b):
    n_phys = cache_fox_k.shape[1]
    pad_lanes = LANES - N_HEADS
    ck = cache_fox_k.transpose(0, 1, 3, 4, 2).reshape(-1, n_phys, D_MODEL, PAGE_SIZE)
    cv = cache_fox_v.transpose(0, 1, 3, 4, 2).reshape(-1, n_phys, D_MODEL, PAGE_SIZE)
    cl = cache_fox_logf.transpose(0, 1, 3, 2)
    dil_caches = [c.transpose(0, 1, 3, 4, 5, 2).reshape(c.shape[0], DEC_BATCH, 2, D_MODEL, c.shape[2])
                  for c in (cache_dil_w128, cache_dil_w512, cache_dil_w2048)]
    cos_p, sin_p = _rope_tables(jnp.arange(SEQ))
    cos_s, sin_s = _rope_tables(jnp.full((1,), PAST_LEN))

    x = jnp.concatenate([x_prompt.reshape(N_PROMPT, D_MODEL), x_sample.reshape(DEC_BATCH, D_MODEL),
                         jnp.zeros((NT - N_PROMPT - DEC_BATCH, D_MODEL), f32)], axis=0)

    fk_p, fk_s, fv_p, fv_s, fl_p, fl_s = [], [], [], [], [], []
    dil_p = [[] for _ in DIL_CONFIGS]
    dil_s = [[] for _ in DIL_CONFIGS]
    for i in range(DEPTH):
        j = i // 2
        if i % 2 == 0:
            w = fox_w_qkv[j].astype(bf16)
            wf = jnp.pad(fox_w_f[j], ((0, 0), (0, pad_lanes)))
            bfv = jnp.pad(fox_b_f[j], (0, pad_lanes))[None, :]
            q, ktb, vb, kt, vt, lft, ct, ccol = fox_proj_prompt(x, w, wf, bfv)
            o_main = fox_attn_prompt(q, ktb, vb, ccol, ct)
            qs, ks, vs, lfs = fox_proj_sample(x, w, wf, bfv)
            o_s = fox_decode(page_table, qs[:, None], ks[:, None], vs[:, None], lfs[:, None], ck, cv, cl, j)
            fk_p.append(kt.reshape(BATCH, N_HEADS, HEAD_DIM, SEQ).transpose(0, 3, 1, 2))
            fv_p.append(vt.reshape(BATCH, N_HEADS, HEAD_DIM, SEQ).transpose(0, 3, 1, 2))
            fl_p.append(lft.transpose(0, 2, 1))
            fk_s.append(ks.reshape(DEC_BATCH, 1, N_HEADS, HEAD_DIM))
            fv_s.append(vs.reshape(DEC_BATCH, 1, N_HEADS, HEAD_DIM))
            fl_s.append(lfs[:, :N_HEADS].reshape(DEC_BATCH, 1, N_HEADS))
            wo = fox_w_o[j].astype(bf16)
        else:
            w = dil_w_qkv[j].astype(bf16)
            os_, ls_ = [], []
            for gi, (win, _) in enumerate(DIL_CONFIGS):
                qr, kr, vr, kvt = dil_proj_prompt(x, w, cos_p, sin_p, gi)
                o_g, l_g = dil_attn_prompt(qr, kr, vr)
                os_.append(o_g)
                ls_.append(l_g)
                keep = min(win, SEQ)
                rows = kvt[:, :, :, SEQ - keep:].reshape(BATCH, 2, N_HEADS, HEAD_DIM, keep)
                dil_p[gi].append(rows.transpose(0, 4, 1, 2, 3))
            o_main = dil_merge_prompt(*os_, *ls_)
            qs, ks, vs = dil_proj_sample(x, w, cos_s, sin_s)
            o_s = dil_decode(qs.transpose(1, 0, 2), ks.transpose(1, 0, 2), vs.transpose(1, 0, 2),
                             *dil_caches, j)
            for gi in range(len(DIL_CONFIGS)):
                dil_s[gi].append(jnp.stack([ks[gi], vs[gi]], axis=1).reshape(DEC_BATCH, 1, 2, N_HEADS, HEAD_DIM))
            wo = dil_w_o[j].astype(bf16)

        o_tail = jnp.pad(o_s.reshape(DEC_BATCH, D_MODEL).astype(bf16), ((0, ROW_TILE - DEC_BATCH), (0, 0)))
        wr = jnp.pad(moe_w_router[i], ((0, 0), (0, LANES - N_EXPERTS)))
        br = jnp.concatenate([moe_b_router[i], jnp.full((LANES - N_EXPERTS,), NEG_INF, f32)])[None, :]
        x1, idx, gate, rank, cnt = post_attn(o_main, o_tail, x, wo, ln_g[i, 0][None, :], ln_b[i, 0][None, :], wr, br)
        ge, nv, nu, idx_arr = _route(idx, rank, cnt)
        y = moe_experts(ge, nv, nu, x1, idx_arr, moe_w1[i], moe_b1[i][:, None, :], moe_w2[i], moe_b2[i][:, None, :])
        x = moe_combine(y, gate, x1, ln_g[i, 1][None, :], ln_b[i, 1][None, :])

    xp = x[:N_PROMPT].reshape(BATCH, SEQ, D_MODEL)
    xs = x[N_PROMPT:N_PROMPT + DEC_BATCH].reshape(DEC_BATCH, 1, D_MODEL)
    return (xp, xs,
            jnp.stack(fk_p), jnp.stack(fk_s), jnp.stack(fv_p), jnp.stack(fv_s), jnp.stack(fl_p), jnp.stack(fl_s),
            jnp.stack(dil_p[0]), jnp.stack(dil_s[0]), jnp.stack(dil_p[1]), jnp.stack(dil_s[1]),
            jnp.stack(dil_p[2]), jnp.stack(dil_s[2]))
```

```python
import functools

import jax
import jax.numpy as jnp
from jax import lax
from jax.experimental import pallas as pl
from jax.experimental.pallas import tpu as pltpu

f32, bf16, i32 = jnp.float32, jnp.bfloat16, jnp.int32

D_MODEL = 1024
BATCH = 8
SEQ = 2048
DEPTH = 4
DEC_BATCH = 32
PAST_LEN = 8192
PAGE_SIZE = 128
HEAD_DIM = 64
N_HEADS = 16
DIL_CONFIGS = ((128, 1), (512, 4), (2048, 16))
N_EXPERTS = 32
TOP_K = 4
SWIGLU_ALPHA = 1.702
SWIGLU_LIMIT = 7.0
ROPE_THETA = 10000.0
LN_EPS = 1e-5
DEEPNORM_ALPHA = (2 * DEPTH) ** 0.25
SCALE = HEAD_DIM ** -0.5
NEG_INF = -1e30

LANES = 128
N_PROMPT = BATCH * SEQ
ROW_TILE = 512
NT = N_PROMPT + ROW_TILE
N_ROW_TILES = NT // ROW_TILE
TILES_PER_SEQ = SEQ // ROW_TILE
MOE_M = 256
MOE_G = NT * TOP_K // MOE_M + N_EXPERTS
N_PAGES = PAST_LEN // PAGE_SIZE
PAGES_PER_STEP = 8
VMEM_LIMIT = 56 * 1024 * 1024

_HI = lax.Precision.HIGHEST


def _cp(sem, vmem=VMEM_LIMIT):
    return pltpu.CompilerParams(dimension_semantics=sem, vmem_limit_bytes=vmem)


def _dot(a, b, precision=None):
    return jnp.dot(a, b, preferred_element_type=f32, precision=precision)


def _dot_nt(a, b):
    return lax.dot_general(a, b, (((1,), (1,)), ((), ())), preferred_element_type=f32)


def _layer_norm(z, g, b):
    mu = jnp.mean(z, axis=-1, keepdims=True)
    zc = z - mu
    var = jnp.mean(zc * zc, axis=-1, keepdims=True)
    return zc * lax.rsqrt(var + LN_EPS) * g + b


def _log_sigmoid(z):
    return jnp.minimum(z, 0.0) - jnp.log1p(jnp.exp(-jnp.abs(z)))


def _fox_proj_kernel(x_ref, w_ref, wf_ref, bf_ref,
                     q_ref, ktb_ref, vb_ref, kt_ref, vt_ref, lft_ref, ct_ref, ccol_ref, carry_ref):
    i = pl.program_id(0)

    @pl.when(i % TILES_PER_SEQ == 0)
    def _():
        carry_ref[...] = jnp.zeros_like(carry_ref)

    x = x_ref[...]
    acc = _dot(x.astype(bf16), w_ref[...])
    q_ref[...] = (acc[:, :D_MODEL] * SCALE).astype(bf16)
    kt = acc[:, D_MODEL:2 * D_MODEL].T
    kt_ref[...] = kt
    ktb_ref[...] = kt.astype(bf16)
    v = acc[:, 2 * D_MODEL:]
    vt_ref[...] = v.T
    vb_ref[...] = v.astype(bf16)
    lf = _log_sigmoid(_dot(x, wf_ref[...], _HI) + bf_ref[...])
    tm = x.shape[0]
    r = lax.broadcasted_iota(i32, (tm, tm), 0)
    c = lax.broadcasted_iota(i32, (tm, tm), 1)
    cs = _dot((c <= r).astype(f32), lf, _HI) + carry_ref[...]
    carry_ref[...] = cs[tm - 1:tm, :]
    ccol_ref[...] = cs
    lft_ref[...] = lf.T[:N_HEADS, :]
    ct_ref[...] = cs.T[:N_HEADS, :]


def fox_proj_prompt(x, w, wf, bfv):
    tm = ROW_TILE
    n = N_PROMPT // tm
    seq_map = lambda i: (i // TILES_PER_SEQ, 0, i % TILES_PER_SEQ)
    row_map = lambda i: (i, 0)
    const = lambda i: (0, 0)
    return pl.pallas_call(
        _fox_proj_kernel,
        grid=(n,),
        in_specs=[pl.BlockSpec((tm, D_MODEL), row_map),
                  pl.BlockSpec((D_MODEL, 3 * D_MODEL), const),
                  pl.BlockSpec((D_MODEL, LANES), const),
                  pl.BlockSpec((1, LANES), const)],
        out_specs=[pl.BlockSpec((tm, D_MODEL), row_map),
                   pl.BlockSpec((None, D_MODEL, tm), seq_map),
                   pl.BlockSpec((tm, D_MODEL), row_map),
                   pl.BlockSpec((None, D_MODEL, tm), seq_map),
                   pl.BlockSpec((None, D_MODEL, tm), seq_map),
                   pl.BlockSpec((None, N_HEADS, tm), seq_map),
                   pl.BlockSpec((None, N_HEADS, tm), seq_map),
                   pl.BlockSpec((tm, LANES), row_map)],
        out_shape=[jax.ShapeDtypeStruct((N_PROMPT, D_MODEL), bf16),
                   jax.ShapeDtypeStruct((BATCH, D_MODEL, SEQ), bf16),
                   jax.ShapeDtypeStruct((N_PROMPT, D_MODEL), bf16),
                   jax.ShapeDtypeStruct((BATCH, D_MODEL, SEQ), f32),
                   jax.ShapeDtypeStruct((BATCH, D_MODEL, SEQ), f32),
                   jax.ShapeDtypeStruct((BATCH, N_HEADS, SEQ), f32),
                   jax.ShapeDtypeStruct((BATCH, N_HEADS, SEQ), f32),
                   jax.ShapeDtypeStruct((N_PROMPT, LANES), f32)],
        scratch_shapes=[pltpu.VMEM((1, LANES), f32)],
        compiler_params=_cp(("arbitrary",)),
        name="fox_proj_prompt",
    )(x, w, wf, bfv)


def _fox_proj_sample_kernel(x_ref, w_ref, wf_ref, bf_ref, q_ref, k_ref, v_ref, lf_ref):
    x = x_ref[...]
    acc = _dot(x.astype(bf16), w_ref[...])
    q_ref[...] = acc[:, :D_MODEL] * SCALE
    k_ref[...] = acc[:, D_MODEL:2 * D_MODEL]
    v_ref[...] = acc[:, 2 * D_MODEL:]
    lf_ref[...] = _log_sigmoid(_dot(x, wf_ref[...], _HI) + bf_ref[...])


def fox_proj_sample(x, w, wf, bfv):
    nb = DEC_BATCH
    const = lambda i: (0, 0)
    return pl.pallas_call(
        _fox_proj_sample_kernel,
        grid=(1,),
        in_specs=[pl.BlockSpec((nb, D_MODEL), lambda i: (N_PROMPT // nb, 0)),
                  pl.BlockSpec((D_MODEL, 3 * D_MODEL), const),
                  pl.BlockSpec((D_MODEL, LANES), const),
                  pl.BlockSpec((1, LANES), const)],
        out_specs=[pl.BlockSpec((nb, D_MODEL), const)] * 3 + [pl.BlockSpec((nb, LANES), const)],
        out_shape=[jax.ShapeDtypeStruct((nb, D_MODEL), f32)] * 3 + [jax.ShapeDtypeStruct((nb, LANES), f32)],
        compiler_params=_cp(("arbitrary",)),
        name="fox_proj_sample",
    )(x, w, wf, bfv)


def _fox_attn_kernel(q_ref, kt_ref, v_ref, ccol_ref, ct_ref, o_ref, m_ref, l_ref, acc_ref, *, bq, bk):
    qi = pl.program_id(1)
    ki = pl.program_id(2)

    @pl.when(ki == 0)
    def _():
        m_ref[...] = jnp.full_like(m_ref, NEG_INF)
        l_ref[...] = jnp.zeros_like(l_ref)
        acc_ref[...] = jnp.zeros_like(acc_ref)

    @pl.when(ki <= qi)
    def _():
        row = qi * bq + lax.broadcasted_iota(i32, (bq, bk), 0)
        col = ki * bk + lax.broadcasted_iota(i32, (bq, bk), 1)
        causal = col <= row
        lo = lax.broadcasted_iota(i32, (1, LANES), 1) < HEAD_DIM
        for hp in range(N_HEADS // 2):
            sl = slice(hp * LANES, (hp + 1) * LANES)
            q2 = q_ref[:, sl]
            kt2 = kt_ref[sl, :]
            v2 = v_ref[:, sl]
            pvs, alphas = [], []
            for par in range(2):
                h = 2 * hp + par
                msk = lo if par == 0 else jnp.logical_not(lo)
                s = _dot(jnp.where(msk, q2, jnp.zeros_like(q2)), kt2)
                s = s + ccol_ref[:, h:h + 1] - ct_ref[h:h + 1, :]
                s = jnp.where(causal, s, NEG_INF)
                m_prev = m_ref[:, h:h + 1]
                m_new = jnp.maximum(m_prev, jnp.max(s, axis=1, keepdims=True))
                alpha = jnp.exp(m_prev - m_new)
                p = jnp.exp(s - m_new)
                l_ref[:, h:h + 1] = alpha * l_ref[:, h:h + 1] + jnp.sum(p, axis=1, keepdims=True)
                m_ref[:, h:h + 1] = m_new
                pvs.append(_dot(p.astype(bf16), jnp.where(msk, v2, jnp.zeros_like(v2))))
                alphas.append(alpha)
            acc_ref[:, sl] = acc_ref[:, sl] * jnp.where(lo, alphas[0], alphas[1]) + pvs[0] + pvs[1]

    @pl.when(ki == qi)
    def _():
        lo = lax.broadcasted_iota(i32, (1, LANES), 1) < HEAD_DIM
        for hp in range(N_HEADS // 2):
            sl = slice(hp * LANES, (hp + 1) * LANES)
            linv = jnp.where(lo, 1.0 / l_ref[:, 2 * hp:2 * hp + 1], 1.0 / l_ref[:, 2 * hp + 1:2 * hp + 2])
            o_ref[:, sl] = (acc_ref[:, sl] * linv).astype(o_ref.dtype)


def fox_attn_prompt(q, kt, v, ccol, ct):
    bq = bk = ROW_TILE
    nq = SEQ // bq
    kmap = lambda b, qi, ki: (b, 0, jnp.minimum(ki, qi))
    return pl.pallas_call(
        functools.partial(_fox_attn_kernel, bq=bq, bk=bk),
        grid=(BATCH, nq, nq),
        in_specs=[pl.BlockSpec((bq, D_MODEL), lambda b, qi, ki: (b * nq + qi, 0)),
                  pl.BlockSpec((None, D_MODEL, bk), kmap),
                  pl.BlockSpec((bk, D_MODEL), lambda b, qi, ki: (b * nq + jnp.minimum(ki, qi), 0)),
                  pl.BlockSpec((bq, LANES), lambda b, qi, ki: (b * nq + qi, 0)),
                  pl.BlockSpec((None, N_HEADS, bk), kmap)],
        out_specs=pl.BlockSpec((bq, D_MODEL), lambda b, qi, ki: (b * nq + qi, 0)),
        out_shape=jax.ShapeDtypeStruct((N_PROMPT, D_MODEL), bf16),
        scratch_shapes=[pltpu.VMEM((bq, LANES), f32), pltpu.VMEM((bq, LANES), f32), pltpu.VMEM((bq, D_MODEL), f32)],
        compiler_params=_cp(("parallel", "arbitrary", "arbitrary")),
        name="fox_attn_prompt",
    )(q, kt, v, ccol, ct)


def _row_to_cols(row):
    return jnp.broadcast_to(row, (LANES, row.shape[1])).T


def _decode_chunk(kt_rows, vt_rows, qt_ref, bias, valid, m_ref, l_ref, acc_ref, tmp_ref, base=0):
    heads = slice(base // HEAD_DIM, base // HEAD_DIM + N_HEADS)
    for h in range(N_HEADS):
        rows = slice(base + h * HEAD_DIM, base + (h + 1) * HEAD_DIM)
        tmp_ref[0, h:h + 1, :] = jnp.sum(kt_rows(h) * qt_ref[rows, :], axis=0, keepdims=True)
    u = tmp_ref[0]
    if bias is not None:
        u = u - bias
    if valid is not None:
        u = jnp.where(valid, u, NEG_INF)
    m_prev = m_ref[heads, :]
    m_new = jnp.maximum(m_prev, u)
    alpha = jnp.exp(m_prev - m_new)
    p = jnp.exp(u - m_new)
    l_ref[heads, :] = alpha * l_ref[heads, :] + p
    m_ref[heads, :] = m_new
    tmp_ref[1] = alpha
    tmp_ref[2] = p
    for h in range(N_HEADS):
        rows = slice(base + h * HEAD_DIM, base + (h + 1) * HEAD_DIM)
        acc_ref[rows, :] = acc_ref[rows, :] * tmp_ref[1, h:h + 1, :] + vt_rows(h) * tmp_ref[2, h:h + 1, :]


def _decode_finish(h, qt_ref, knt, vnt, shift, m_ref, l_ref, acc_ref, base=0):
    rows = slice(base + h * HEAD_DIM, base + (h + 1) * HEAD_DIM)
    hrow = slice(base // HEAD_DIM + h, base // HEAD_DIM + h + 1)
    m_vec = m_ref[hrow, :]
    m_past = jnp.max(m_vec, axis=1, keepdims=True)
    sc = jnp.exp(m_vec - m_past)
    l_past = jnp.sum(l_ref[hrow, :] * sc, axis=1, keepdims=True)
    o_past = jnp.sum(acc_ref[rows, :] * sc, axis=1, keepdims=True)
    s_new = jnp.sum(qt_ref[rows, :] * knt[h * HEAD_DIM:(h + 1) * HEAD_DIM, :], axis=0, keepdims=True)
    m_sh = m_past + shift
    m_f = jnp.maximum(m_sh, s_new)
    a = jnp.exp(m_sh - m_f)
    bn = jnp.exp(s_new - m_f)
    l_f = l_past * a + bn
    o = (o_past * a + bn * vnt[h * HEAD_DIM:(h + 1) * HEAD_DIM, :]) / l_f
    return o, m_f + jnp.log(l_f)


def _fox_decode_kernel(pt_ref, q_ref, kn_ref, vn_ref, lfn_ref, *rest):
    np_ = PAGES_PER_STEP
    k_refs, v_refs, lf_refs = rest[:np_], rest[np_:2 * np_], rest[2 * np_:3 * np_]
    o_ref, qt_ref, m_ref, l_ref, acc_ref, carry_ref, ocol_ref, tmp_ref = rest[3 * np_:]
    s = pl.program_id(1)

    def head_rows(ref):
        return lambda h: ref[h * HEAD_DIM:(h + 1) * HEAD_DIM, :]

    @pl.when(s == 0)
    def _():
        qt_ref[...] = _row_to_cols(q_ref[...])
        m_ref[...] = jnp.full_like(m_ref, NEG_INF)
        l_ref[...] = jnp.zeros_like(l_ref)
        acc_ref[...] = jnp.zeros_like(acc_ref)
        carry_ref[...] = jnp.zeros_like(carry_ref)

    r = lax.broadcasted_iota(i32, (LANES, LANES), 0)
    c = lax.broadcasted_iota(i32, (LANES, LANES), 1)
    tri = (r <= c).astype(f32)
    for t in range(np_):
        lf = lf_refs[t][...]
        pre = _dot(lf, tri, _HI) + carry_ref[...]
        carry_ref[...] = carry_ref[...] + jnp.sum(lf, axis=1, keepdims=True)
        _decode_chunk(head_rows(k_refs[t]), head_rows(v_refs[t]), qt_ref, pre, None, m_ref, l_ref, acc_ref, tmp_ref)

    @pl.when(s == pl.num_programs(1) - 1)
    def _():
        knt = _row_to_cols(kn_ref[...])
        vnt = _row_to_cols(vn_ref[...])
        for h in range(N_HEADS):
            shift = carry_ref[h:h + 1, :] + lfn_ref[:, h:h + 1]
            o, _ = _decode_finish(h, qt_ref, knt, vnt, shift, m_ref, l_ref, acc_ref)
            ocol_ref[h * HEAD_DIM:(h + 1) * HEAD_DIM, :] = jnp.broadcast_to(o, (HEAD_DIM, LANES))
        o_ref[...] = ocol_ref[...].T[0:1, :]


def fox_decode(page_table, q, kn, vn, lfn, ck, cv, cl, layer):
    np_ = PAGES_PER_STEP
    row = lambda b, s, pt: (b, 0, 0)

    def page_map(t):
        return lambda b, s, pt: (layer, pt[b, s * np_ + t], 0, 0)

    kv_specs = [pl.BlockSpec((None, None, D_MODEL, PAGE_SIZE), page_map(t)) for t in range(np_)]
    lf_specs = [pl.BlockSpec((None, None, N_HEADS, PAGE_SIZE), page_map(t)) for t in range(np_)]
    grid_spec = pltpu.PrefetchScalarGridSpec(
        num_scalar_prefetch=1,
        grid=(DEC_BATCH, N_PAGES // np_),
        in_specs=[pl.BlockSpec((None, 1, D_MODEL), row)] * 3 + [pl.BlockSpec((None, 1, LANES), row)]
                 + kv_specs + kv_specs + lf_specs,
        out_specs=pl.BlockSpec((None, 1, D_MODEL), row),
        scratch_shapes=[pltpu.VMEM((D_MODEL, LANES), f32),
                        pltpu.VMEM((N_HEADS, LANES), f32), pltpu.VMEM((N_HEADS, LANES), f32),
                        pltpu.VMEM((D_MODEL, LANES), f32), pltpu.VMEM((N_HEADS, LANES), f32),
                        pltpu.VMEM((D_MODEL, LANES), f32), pltpu.VMEM((3, N_HEADS, LANES), f32)])
    return pl.pallas_call(
        _fox_decode_kernel,
        grid_spec=grid_spec,
        out_shape=jax.ShapeDtypeStruct((DEC_BATCH, 1, D_MODEL), f32),
        compiler_params=_cp(("parallel", "arbitrary")),
        name="fox_decode",
    )(page_table, q, kn, vn, lfn, *([ck] * np_), *([cv] * np_), *([cl] * np_))


def _post_attn_kernel(o_ref, ot_ref, x_ref, wo_ref, g_ref, b_ref, wr_ref, br_ref,
                      x1_ref, x1t_ref, idx_ref, gate_ref, rank_ref, cnt_ref, carry_ref, *, n_main):
    i = pl.program_id(0)

    @pl.when(i == 0)
    def _():
        carry_ref[...] = jnp.zeros_like(carry_ref)

    o = jnp.where(i < n_main, o_ref[...], ot_ref[...])
    x1 = _layer_norm(DEEPNORM_ALPHA * x_ref[...] + _dot(o, wo_ref[...]), g_ref[...], b_ref[...])
    x1_ref[...] = x1
    nblk = D_MODEL // LANES
    for c in range(nblk):
        x1t_ref[pl.ds(c, x1.shape[0], stride=nblk), :] = x1[:, c * LANES:(c + 1) * LANES]

    tm = x1.shape[0]
    logits = _dot(x1, wr_ref[...], _HI) + br_ref[...]
    lane = lax.broadcasted_iota(i32, (tm, LANES), 1)
    lane_f = lane.astype(f32)
    lg = logits
    onehots, vals, idxs = [], [], []
    for _ in range(TOP_K):
        mx = jnp.max(lg, axis=1, keepdims=True)
        ix = jnp.min(jnp.where(lg == mx, lane_f, float(LANES)), axis=1, keepdims=True)
        oh = lane_f == ix
        lg = jnp.where(oh, -3e38, lg)
        onehots.append(oh)
        vals.append(mx)
        idxs.append(ix)
    es = [jnp.exp(v - vals[0]) for v in vals]
    den = es[0] + es[1] + es[2] + es[3]
    chosen = jnp.zeros((tm, LANES), f32)
    for oh in onehots:
        chosen = chosen + oh.astype(f32)
    r = lax.broadcasted_iota(i32, (tm, tm), 0)
    c = lax.broadcasted_iota(i32, (tm, tm), 1)
    before = _dot((c < r).astype(bf16), chosen.astype(bf16)) + carry_ref[...]
    carry_ref[...] = carry_ref[...] + jnp.sum(chosen, axis=0, keepdims=True)
    cnt_ref[...] = carry_ref[...]
    idx_o = jnp.zeros((tm, LANES), f32)
    gate_o = jnp.zeros((tm, LANES), f32)
    rank_o = jnp.zeros((tm, LANES), f32)
    for k in range(TOP_K):
        sel = lane == k
        rk = jnp.sum(jnp.where(onehots[k], before, 0.0), axis=1, keepdims=True)
        idx_o = jnp.where(sel, idxs[k], idx_o)
        gate_o = jnp.where(sel, es[k] / den, gate_o)
        rank_o = jnp.where(sel, rk, rank_o)
    idx_ref[...] = idx_o.astype(i32)
    gate_ref[...] = gate_o
    rank_ref[...] = rank_o.astype(i32)


def post_attn(o_main, o_tail, x, wo, g, b, wr, br):
    tm = ROW_TILE
    n_main = N_PROMPT // tm
    row = lambda i: (i, 0)
    const = lambda i: (0, 0)
    return pl.pallas_call(
        functools.partial(_post_attn_kernel, n_main=n_main),
        grid=(N_ROW_TILES,),
        in_specs=[pl.BlockSpec((tm, D_MODEL), lambda i: (jnp.minimum(i, n_main - 1), 0)),
                  pl.BlockSpec((tm, D_MODEL), const),
                  pl.BlockSpec((tm, D_MODEL), row),
                  pl.BlockSpec((D_MODEL, D_MODEL), const),
                  pl.BlockSpec((1, D_MODEL), const),
                  pl.BlockSpec((1, D_MODEL), const),
                  pl.BlockSpec((D_MODEL, LANES), const),
                  pl.BlockSpec((1, LANES), const)],
        out_specs=[pl.BlockSpec((tm, D_MODEL), row),
                   pl.BlockSpec((tm * D_MODEL // LANES, LANES), row),
                   pl.BlockSpec((tm, LANES), row),
                   pl.BlockSpec((tm, LANES), row),
                   pl.BlockSpec((tm, LANES), row),
                   pl.BlockSpec((1, LANES), const)],
        out_shape=[jax.ShapeDtypeStruct((NT, D_MODEL), f32),
                   jax.ShapeDtypeStruct((NT * D_MODEL // LANES, LANES), f32),
                   jax.ShapeDtypeStruct((NT, LANES), i32),
                   jax.ShapeDtypeStruct((NT, LANES), f32),
                   jax.ShapeDtypeStruct((NT, LANES), i32),
                   jax.ShapeDtypeStruct((1, LANES), f32)],
        scratch_shapes=[pltpu.VMEM((1, LANES), f32)],
        compiler_params=_cp(("arbitrary",)),
        name="post_attn",
    )(o_main, o_tail, x, wo, g, b, wr, br)


def _moe_kernel(ge_ref, nv_ref, nu_ref, x_hbm, idx_hbm, w1_ref, b1_ref, w2_ref, b2_ref, y_hbm,
                xbuf, ybuf, idx_smem, w1b, w2b, gsem, ssem, isem):
    g = pl.program_id(0)
    nu = nu_ref[0]
    m = MOE_M
    nblk = D_MODEL // LANES

    def idx_copy(gg, sl):
        return pltpu.make_async_copy(idx_hbm.at[gg], idx_smem.at[sl], isem.at[sl])

    def gather_row(sl, j):
        src = pl.multiple_of(idx_smem[sl, 0, j], 8)
        return pltpu.make_async_copy(x_hbm.at[pl.ds(src, 8)], xbuf.at[sl, pl.ds(j * 8, 8)], gsem.at[sl])

    def scatter_row(sl, j):
        dst = pl.multiple_of(idx_smem[sl, 1, j], 8)
        return pltpu.make_async_copy(ybuf.at[sl, pl.ds(j * 8, 8)], y_hbm.at[pl.ds(dst, 8)], ssem.at[sl])

    def start_rows(row_copy, sl, n):
        @pl.when(n == m)
        def _():
            for j in range(m):
                row_copy(sl, j).start()

        @pl.when(n < m)
        def _():
            def body(j, carry):
                row_copy(sl, j).start()
                return carry
            lax.fori_loop(0, n, body, 0)

    def wait_rows(n, wait_static):
        @pl.when(n == m)
        def _():
            wait_static(m)

        @pl.when(n < m)
        def _():
            bit = m // 2
            while bit >= 1:
                @pl.when((n & bit) != 0)
                def _(bit=bit):
                    wait_static(bit)
                bit //= 2

    def wait_gathers(sl, n):
        wait_rows(n, lambda k: pltpu.make_async_copy(
            x_hbm.at[pl.ds(0, 8 * k)], xbuf.at[sl, pl.ds(0, 8 * k)], gsem.at[sl]).wait())

    def wait_scatters(sl, n):
        wait_rows(n, lambda k: pltpu.make_async_copy(
            ybuf.at[sl, pl.ds(0, 8 * k)], y_hbm.at[pl.ds(0, 8 * k)], ssem.at[sl]).wait())

    def step(slot):
        other = 1 - slot

        @pl.when(g + 1 < nu)
        def _():
            idx_copy(g + 1, other).wait()
            start_rows(gather_row, other, nv_ref[g + 1])

        @pl.when(g >= 2)
        def _():
            wait_scatters(slot, nv_ref[jnp.maximum(g - 2, 0)])

        wait_gathers(slot, nv_ref[g])

        @pl.when(jnp.logical_or(g == 0, ge_ref[g] != ge_ref[jnp.maximum(g - 1, 0)]))
        def _():
            w1b[...] = w1_ref[...].astype(bf16)
            w2b[...] = w2_ref[...].astype(bf16)

        x = jnp.concatenate([xbuf[slot, pl.ds(c, m, stride=nblk), :] for c in range(nblk)], axis=1).astype(bf16)
        h = _dot(x, w1b[...]) + b1_ref[...]
        glu = jnp.minimum(h[:, :D_MODEL], SWIGLU_LIMIT)
        lin = jnp.clip(h[:, D_MODEL:], -SWIGLU_LIMIT, SWIGLU_LIMIT)
        act = glu * (1.0 / (1.0 + jnp.exp(-SWIGLU_ALPHA * glu))) * (lin + 1.0)
        y = _dot(act.astype(bf16), w2b[...]) + b2_ref[...]
        for c in range(nblk):
            ybuf[slot, pl.ds(c, m, stride=nblk), :] = y[:, c * LANES:(c + 1) * LANES]

        start_rows(scatter_row, slot, nv_ref[g])

        @pl.when(g + 2 < nu)
        def _():
            idx_copy(g + 2, slot).start()

        @pl.when(g == nu - 1)
        def _():
            wait_scatters(slot, nv_ref[g])

            @pl.when(g >= 1)
            def _():
                wait_scatters(other, nv_ref[jnp.maximum(g - 1, 0)])

    @pl.when(g < nu)
    def _():
        @pl.when(g == 0)
        def _():
            xbuf[...] = jnp.zeros_like(xbuf)
            first = idx_copy(0, 0)
            first.start()
            first.wait()
            start_rows(gather_row, 0, nv_ref[0])

            @pl.when(nu > 1)
            def _():
                idx_copy(1, 1).start()

        @pl.when(g % 2 == 0)
        def _():
            step(0)

        @pl.when(g % 2 == 1)
        def _():
            step(1)


def moe_experts(ge, nv, nu, x, idx, w1, b1, w2, b2, layer):
    m = MOE_M
    wmap = lambda g, ge, nv, nu: (layer, ge[g], 0, 0)
    grid_spec = pltpu.PrefetchScalarGridSpec(
        num_scalar_prefetch=3,
        grid=(MOE_G,),
        in_specs=[pl.BlockSpec(memory_space=pl.ANY),
                  pl.BlockSpec(memory_space=pl.ANY),
                  pl.BlockSpec((None, None, D_MODEL, 2 * D_MODEL), wmap),
                  pl.BlockSpec((None, None, 1, 2 * D_MODEL), wmap),
                  pl.BlockSpec((None, None, D_MODEL, D_MODEL), wmap),
                  pl.BlockSpec((None, None, 1, D_MODEL), wmap)],
        out_specs=pl.BlockSpec(memory_space=pl.ANY),
        scratch_shapes=[pltpu.VMEM((2, m * D_MODEL // LANES, LANES), f32),
                        pltpu.VMEM((2, m * D_MODEL // LANES, LANES), f32),
                        pltpu.SMEM((2, 2, m), i32),
                        pltpu.VMEM((D_MODEL, 2 * D_MODEL), bf16),
                        pltpu.VMEM((D_MODEL, D_MODEL), bf16),
                        pltpu.SemaphoreType.DMA((2,)),
                        pltpu.SemaphoreType.DMA((2,)),
                        pltpu.SemaphoreType.DMA((2,))])
    return pl.pallas_call(
        _moe_kernel,
        grid_spec=grid_spec,
        out_shape=jax.ShapeDtypeStruct((TOP_K * NT * D_MODEL // LANES, LANES), f32),
        compiler_params=_cp(("arbitrary",)),
        name="moe_experts",
    )(ge, nv, nu, x, idx, w1, b1, w2, b2)


def _route(idx, rank, cnt):
    m, g = MOE_M, MOE_G
    idx4 = idx[:, :TOP_K]
    rank4 = rank[:, :TOP_K]
    counts = cnt[0, :N_EXPERTS].astype(i32)
    padded = (counts + m - 1) // m * m
    pad_end = jnp.cumsum(padded)
    pad_start = pad_end - padded
    dest = pad_start[idx4] + rank4
    assign = jnp.arange(NT, dtype=i32)[:, None] + NT * jnp.arange(TOP_K, dtype=i32)[None, :]
    rows = jnp.zeros((g * m,), i32).at[dest.reshape(-1)].set(assign.reshape(-1), unique_indices=True)
    per_tok = D_MODEL // LANES
    idx_arr = jnp.stack([rows % NT * per_tok, rows * per_tok], axis=0).reshape(2, g, m).transpose(1, 0, 2)
    gstart = jnp.arange(g, dtype=i32) * m
    ge = jnp.minimum(jnp.sum((gstart[:, None] >= pad_end[None, :]).astype(i32), axis=1), N_EXPERTS - 1)
    nv = jnp.clip(counts[ge] - (gstart - pad_start[ge]), 0, m).astype(i32)
    nu = (pad_end[-1:] // m).astype(i32)
    return ge, nv, nu, idx_arr


def _moe_combine_kernel(y0, y1, y2, y3, gate_ref, x_ref, g_ref, b_ref, o_ref):
    gt = gate_ref[...]
    cols = []
    nblk = D_MODEL // LANES
    tm = x_ref.shape[0]
    for c in range(nblk):
        rows = pl.ds(c, tm, stride=nblk)
        cols.append(gt[:, 0:1] * y0[rows, :] + gt[:, 1:2] * y1[rows, :] + gt[:, 2:3] * y2[rows, :] + gt[:, 3:4] * y3[rows, :])
    y = jnp.concatenate(cols, axis=1)
    o_ref[...] = _layer_norm(DEEPNORM_ALPHA * x_ref[...] + y, g_ref[...], b_ref[...])


def moe_combine(y, gate, x, g, b):
    tm = ROW_TILE
    row = lambda i: (i, 0)
    const = lambda i: (0, 0)
    y_specs = [pl.BlockSpec((tm * D_MODEL // LANES, LANES), (lambda i, k=k: (k * N_ROW_TILES + i, 0))) for k in range(TOP_K)]
    return pl.pallas_call(
        _moe_combine_kernel,
        grid=(N_ROW_TILES,),
        in_specs=y_specs + [pl.BlockSpec((tm, LANES), row), pl.BlockSpec((tm, D_MODEL), row),
                            pl.BlockSpec((1, D_MODEL), const), pl.BlockSpec((1, D_MODEL), const)],
        out_specs=pl.BlockSpec((tm, D_MODEL), row),
        out_shape=jax.ShapeDtypeStruct((NT, D_MODEL), f32),
        compiler_params=_cp(("parallel",)),
        name="moe_combine",
    )(y, y, y, y, gate, x, g, b)


def _rope(xb, cos, sin, lo32):
    partner = jnp.where(lo32, pltpu.roll(xb, LANES - HEAD_DIM // 2, 1), pltpu.roll(xb, HEAD_DIM // 2, 1))
    return xb * cos + partner * sin


def _rope_tables(pos):
    half = HEAD_DIM // 2
    lane = jnp.arange(LANES)
    inv = ROPE_THETA ** (-(lane % half).astype(f32) / half)
    ang = pos.astype(f32)[:, None] * inv[None, :]
    sign = jnp.where((lane % HEAD_DIM) < half, -1.0, 1.0).astype(f32)
    return jnp.cos(ang), jnp.sin(ang) * sign[None, :]


def _dil_proj_kernel(x_ref, w_ref, cos_ref, sin_ref, q_ref, k_ref, v_ref, kvt_ref, *scratch, dil):
    tm = x_ref.shape[0]
    nblk = D_MODEL // LANES
    acc = _dot(x_ref[...].astype(bf16), w_ref[...])
    cos, sin = cos_ref[...], sin_ref[...]
    lo32 = (lax.broadcasted_iota(i32, (1, LANES), 1) % HEAD_DIM) < HEAD_DIM // 2
    pieces = []
    for cb in range(3 * nblk):
        blk = acc[:, cb * LANES:(cb + 1) * LANES]
        if cb < nblk:
            blk = _rope(blk * SCALE, cos, sin, lo32)
        elif cb < 2 * nblk:
            blk = _rope(blk, cos, sin, lo32)
        pieces.append(blk)
    kvt_ref[0] = jnp.concatenate(pieces[nblk:2 * nblk], axis=1).T
    kvt_ref[1] = jnp.concatenate(pieces[2 * nblk:], axis=1).T
    outs = (q_ref, k_ref, v_ref)
    if dil == 1:
        for cb in range(3 * nblk):
            outs[cb // nblk][0, :, (cb % nblk) * LANES:(cb % nblk + 1) * LANES] = pieces[cb].astype(bf16)
    else:
        slab = scratch[0]
        for cb in range(3 * nblk):
            slab[cb] = pieces[cb]
        n = tm // dil
        for cb in range(3 * nblk):
            for r in range(dil):
                outs[cb // nblk][r, :, (cb % nblk) * LANES:(cb % nblk + 1) * LANES] = (
                    slab[cb, pl.ds(r, n, stride=dil), :].astype(bf16))


def dil_proj_prompt(x, w, cos, sin, group):
    dil = DIL_CONFIGS[group][1]
    tm = ROW_TILE
    lu = SEQ // dil
    n = N_PROMPT // tm
    res_map = lambda i: (i // TILES_PER_SEQ, 0, i % TILES_PER_SEQ, 0)
    res_spec = pl.BlockSpec((None, dil, tm // dil, D_MODEL), res_map)
    res_shape = jax.ShapeDtypeStruct((BATCH, dil, lu, D_MODEL), bf16)
    return pl.pallas_call(
        functools.partial(_dil_proj_kernel, dil=dil),
        grid=(n,),
        in_specs=[pl.BlockSpec((tm, D_MODEL), lambda i: (i, 0)),
                  pl.BlockSpec((D_MODEL, 3 * D_MODEL), lambda i: (0, group)),
                  pl.BlockSpec((tm, LANES), lambda i: (i % TILES_PER_SEQ, 0)),
                  pl.BlockSpec((tm, LANES), lambda i: (i % TILES_PER_SEQ, 0))],
        out_specs=[res_spec, res_spec, res_spec,
                   pl.BlockSpec((None, 2, D_MODEL, tm), lambda i: (i // TILES_PER_SEQ, 0, 0, i % TILES_PER_SEQ))],
        out_shape=[res_shape, res_shape, res_shape, jax.ShapeDtypeStruct((BATCH, 2, D_MODEL, SEQ), f32)],
        scratch_shapes=[] if dil == 1 else [pltpu.VMEM((3 * D_MODEL // LANES, tm, LANES), f32)],
        compiler_params=_cp(("parallel",)),
        name=f"dil_proj_prompt_g{group}",
    )(x, w, cos, sin)


def _dil_proj_sample_kernel(x_ref, w_ref, cos_ref, sin_ref, q_ref, k_ref, v_ref):
    nblk = D_MODEL // LANES
    acc = _dot(x_ref[...].astype(bf16), w_ref[...])
    cos, sin = cos_ref[...], sin_ref[...]
    lo32 = (lax.broadcasted_iota(i32, (1, LANES), 1) % HEAD_DIM) < HEAD_DIM // 2
    for gi in range(len(DIL_CONFIGS)):
        for cb in range(3 * nblk):
            col = gi * 3 * D_MODEL + cb * LANES
            blk = acc[:, col:col + LANES]
            dst = slice((cb % nblk) * LANES, (cb % nblk + 1) * LANES)
            if cb < nblk:
                q_ref[gi, :, dst] = _rope(blk * SCALE, cos, sin, lo32)
            elif cb < 2 * nblk:
                k_ref[gi, :, dst] = _rope(blk, cos, sin, lo32)
            else:
                v_ref[gi, :, dst] = blk


def dil_proj_sample(x, w, cos, sin):
    nb = DEC_BATCH
    ng = len(DIL_CONFIGS)
    const = lambda i: (0, 0)
    out_spec = pl.BlockSpec((ng, nb, D_MODEL), lambda i: (0, 0, 0))
    out_shape = jax.ShapeDtypeStruct((ng, nb, D_MODEL), f32)
    return pl.pallas_call(
        _dil_proj_sample_kernel,
        grid=(1,),
        in_specs=[pl.BlockSpec((nb, D_MODEL), lambda i: (N_PROMPT // nb, 0)),
                  pl.BlockSpec((D_MODEL, ng * 3 * D_MODEL), const),
                  pl.BlockSpec((1, LANES), const),
                  pl.BlockSpec((1, LANES), const)],
        out_specs=[out_spec] * 3,
        out_shape=[out_shape] * 3,
        compiler_params=_cp(("arbitrary",)),
        name="dil_proj_sample",
    )(x, w, cos, sin)


def _dil_attn_kernel(q_ref, kc_ref, kp_ref, vc_ref, vp_ref, o_ref, lse_ref, s_ref, p_ref):
    n = pl.program_id(2)
    blk = q_ref.shape[0]
    lane = lax.broadcasted_iota(i32, (1, LANES), 1)
    lo = lane < HEAD_DIM
    halves = (lo, jnp.logical_not(lo))
    for hp in range(N_HEADS // 2):
        sl = slice(hp * LANES, (hp + 1) * LANES)
        q2 = q_ref[:, sl]
        kk = jnp.concatenate([kp_ref[:, sl], kc_ref[:, sl]], axis=0)
        for par in range(2):
            h = 2 * hp + par
            s_ref[h * blk:(h + 1) * blk, :] = _dot_nt(jnp.where(halves[par], q2, jnp.zeros_like(q2)), kk)

    shape = (N_HEADS * blk, 2 * blk)
    i = lax.broadcasted_iota(i32, shape, 0) % blk
    j = lax.broadcasted_iota(i32, shape, 1)
    ok = jnp.logical_or(jnp.logical_and(jnp.logical_and(j < blk, j >= i), n > 0),
                        jnp.logical_and(j >= blk, j - blk <= i))
    s = jnp.where(ok, s_ref[...], NEG_INF)
    mx = jnp.max(s, axis=1, keepdims=True)
    p = jnp.exp(s - mx)
    l = jnp.sum(p, axis=1, keepdims=True)
    p_ref[...] = p.astype(bf16)
    linv = 1.0 / l
    lse_col = mx + jnp.log(l)

    lse = jnp.zeros((blk, LANES), f32)
    for hp in range(N_HEADS // 2):
        sl = slice(hp * LANES, (hp + 1) * LANES)
        vv = jnp.concatenate([vp_ref[:, sl], vc_ref[:, sl]], axis=0)
        o2 = jnp.zeros((blk, LANES), f32)
        for par in range(2):
            h = 2 * hp + par
            rows = slice(h * blk, (h + 1) * blk)
            o2 = o2 + _dot(p_ref[rows, :], jnp.where(halves[par], vv, jnp.zeros_like(vv))) * linv[rows, :]
            lse = jnp.where(lane == h, lse_col[rows, :], lse)
        o_ref[:, sl] = o2
    lse_ref[...] = lse


def dil_attn_prompt(q, k, v):
    _, dil, lu, _ = q.shape
    blk = 128
    nb = lu // blk
    cur = lambda b, r, n: (b, r, n, 0)
    prev = lambda b, r, n: (b, r, jnp.maximum(n - 1, 0), 0)
    spec = lambda m: pl.BlockSpec((None, None, blk, D_MODEL), m)
    return pl.pallas_call(
        _dil_attn_kernel,
        grid=(BATCH, dil, nb),
        in_specs=[spec(cur), spec(cur), spec(prev), spec(cur), spec(prev)],
        out_specs=[spec(cur), pl.BlockSpec((None, None, blk, LANES), cur)],
        out_shape=[jax.ShapeDtypeStruct((BATCH, dil, lu, D_MODEL), f32),
                   jax.ShapeDtypeStruct((BATCH, dil, lu, LANES), f32)],
        scratch_shapes=[pltpu.VMEM((N_HEADS * blk, 2 * blk), f32), pltpu.VMEM((N_HEADS * blk, 2 * blk), bf16)],
        compiler_params=_cp(("parallel", "parallel", "arbitrary")),
        name=f"dil_attn_prompt_d{dil}",
    )(q, k, k, v, v)


def _expand_heads(w):
    hrow = lax.broadcasted_iota(i32, (LANES, D_MODEL), 0)
    col = lax.broadcasted_iota(i32, (LANES, D_MODEL), 1)
    e = (col // HEAD_DIM == hrow).astype(bf16)
    hi = w.astype(bf16)
    lo = (w - hi.astype(f32)).astype(bf16)
    return _dot(hi, e) + _dot(lo, e)


def _dil_merge_kernel(o1_ref, o2_ref, o3_ref, l1_ref, l2_ref, l3_ref, out_ref, nat_o, nat_l):
    tm = out_ref.shape[0]
    nblk = D_MODEL // LANES
    for gi, (o_ref, l_ref) in enumerate(((o2_ref, l2_ref), (o3_ref, l3_ref))):
        dil = DIL_CONFIGS[gi + 1][1]
        n = tm // dil
        for r in range(dil):
            nat_l[gi, pl.ds(r, n, stride=dil), :] = l_ref[r]
            for cb in range(nblk):
                nat_o[gi, cb, pl.ds(r, n, stride=dil), :] = o_ref[r, :, cb * LANES:(cb + 1) * LANES]
    lses = (l1_ref[0], nat_l[0], nat_l[1])
    mx = jnp.maximum(jnp.maximum(lses[0], lses[1]), lses[2])
    es = [jnp.exp(l - mx) for l in lses]
    den = es[0] + es[1] + es[2]
    ws = [_expand_heads(e / den) for e in es]
    for cb in range(nblk):
        sl = slice(cb * LANES, (cb + 1) * LANES)
        out_ref[:, sl] = (ws[0][:, sl] * o1_ref[0, :, sl] + ws[1][:, sl] * nat_o[0, cb]
                          + ws[2][:, sl] * nat_o[1, cb]).astype(out_ref.dtype)


def dil_merge_prompt(o1, o2, o3, l1, l2, l3):
    tm = ROW_TILE
    res_map = lambda b, t: (b, 0, t, 0)

    def ospec(dil, width):
        return pl.BlockSpec((None, dil, tm // dil, width), res_map)

    dils = [d for _, d in DIL_CONFIGS]
    return pl.pallas_call(
        _dil_merge_kernel,
        grid=(BATCH, TILES_PER_SEQ),
        in_specs=[ospec(d, D_MODEL) for d in dils] + [ospec(d, LANES) for d in dils],
        out_specs=pl.BlockSpec((tm, D_MODEL), lambda b, t: (b * TILES_PER_SEQ + t, 0)),
        out_shape=jax.ShapeDtypeStruct((N_PROMPT, D_MODEL), bf16),
        scratch_shapes=[pltpu.VMEM((2, D_MODEL // LANES, tm, LANES), f32), pltpu.VMEM((2, tm, LANES), f32)],
        compiler_params=_cp(("parallel", "parallel")),
        name="dil_merge_prompt",
    )(o1, o2, o3, l1, l2, l3)


def _dil_decode_kernel(q_ref, kn_ref, vn_ref, c1_ref, c2_ref, c3_ref, o_ref,
                       qt_ref, m_ref, l_ref, acc_ref, ocol_ref, tmp_ref):
    c = pl.program_id(1)
    ng = len(DIL_CONFIGS)
    lane = lax.broadcasted_iota(i32, (1, LANES), 1)

    def group_chunks(gi, cache_ref, n_chunks, pos0):
        dil = DIL_CONFIGS[gi][1]
        for t in range(n_chunks):
            valid = None if dil == 1 else ((pos0 + t * LANES + lane) % dil) == 0

            def slab(kv, t=t):
                return lambda h: cache_ref[kv, h * HEAD_DIM:(h + 1) * HEAD_DIM, t * LANES:(t + 1) * LANES]

            _decode_chunk(slab(0), slab(1), qt_ref, None, valid, m_ref, l_ref, acc_ref, tmp_ref, base=gi * D_MODEL)

    @pl.when(c == 0)
    def _():
        for gi in range(ng):
            qt_ref[gi * D_MODEL:(gi + 1) * D_MODEL, :] = _row_to_cols(q_ref[gi:gi + 1, :])
        m_ref[...] = jnp.full_like(m_ref, NEG_INF)
        l_ref[...] = jnp.zeros_like(l_ref)
        acc_ref[...] = jnp.zeros_like(acc_ref)
        group_chunks(0, c1_ref, c1_ref.shape[2] // LANES, 0)
        group_chunks(1, c2_ref, c2_ref.shape[2] // LANES, 0)

    group_chunks(2, c3_ref, c3_ref.shape[2] // LANES, c * c3_ref.shape[2])

    @pl.when(c == pl.num_programs(1) - 1)
    def _():
        knts = [_row_to_cols(kn_ref[gi:gi + 1, :]) for gi in range(ng)]
        vnts = [_row_to_cols(vn_ref[gi:gi + 1, :]) for gi in range(ng)]
        for h in range(N_HEADS):
            outs = [_decode_finish(h, qt_ref, knts[gi], vnts[gi], 0.0, m_ref, l_ref, acc_ref, base=gi * D_MODEL)
                    for gi in range(ng)]
            mx = jnp.maximum(jnp.maximum(outs[0][1], outs[1][1]), outs[2][1])
            es = [jnp.exp(o[1] - mx) for o in outs]
            den = es[0] + es[1] + es[2]
            merged = (es[0] * outs[0][0] + es[1] * outs[1][0] + es[2] * outs[2][0]) / den
            ocol_ref[h * HEAD_DIM:(h + 1) * HEAD_DIM, :] = jnp.broadcast_to(merged, (HEAD_DIM, LANES))
        o_ref[...] = ocol_ref[...].T[0:1, :]


def dil_decode(q, kn, vn, c1, c2, c3, layer):
    ng = len(DIL_CONFIGS)
    chunk = 512
    n_chunks = c3.shape[-1] // chunk
    row = lambda b, c: (b, 0, 0)
    whole = lambda b, c: (layer, b, 0, 0, 0)
    return pl.pallas_call(
        _dil_decode_kernel,
        grid=(DEC_BATCH, n_chunks),
        in_specs=[pl.BlockSpec((None, ng, D_MODEL), row)] * 3
                 + [pl.BlockSpec((None, None, 2, D_MODEL, c1.shape[-1]), whole),
                    pl.BlockSpec((None, None, 2, D_MODEL, c2.shape[-1]), whole),
                    pl.BlockSpec((None, None, 2, D_MODEL, chunk), lambda b, c: (layer, b, 0, 0, c))],
        out_specs=pl.BlockSpec((None, 1, D_MODEL), row),
        out_shape=jax.ShapeDtypeStruct((DEC_BATCH, 1, D_MODEL), f32),
        scratch_shapes=[pltpu.VMEM((ng * D_MODEL, LANES), f32),
                        pltpu.VMEM((ng * N_HEADS, LANES), f32), pltpu.VMEM((ng * N_HEADS, LANES), f32),
                        pltpu.VMEM((ng * D_MODEL, LANES), f32), pltpu.VMEM((D_MODEL, LANES), f32),
                        pltpu.VMEM((3, N_HEADS, LANES), f32)],
        compiler_params=_cp(("parallel", "arbitrary")),
        name="dil_decode",
    )(q, kn, vn, c1, c2, c3)


def kernel(x_prompt, x_sample, cache_fox_k, cache_fox_v, cache_fox_logf, cache_dil_w128, cache_dil_w512,
           cache_dil_w2048, page_table, fox_w_qkv, fox_w_f, fox_b_f, fox_w_o, dil_w_qkv, dil_w_o,
           moe_w_router, moe_b_router, moe_w1, moe_b1, moe_w2, moe_b2, ln_g, ln_b):
    n_phys = cache_fox_k.shape[1]
    pad_lanes = LANES - N_HEADS
    ck = cache_fox_k.transpose(0, 1, 3, 4, 2).reshape(-1, n_phys, D_MODEL, PAGE_SIZE)
    cv = cache_fox_v.transpose(0, 1, 3, 4, 2).reshape(-1, n_phys, D_MODEL, PAGE_SIZE)
    cl = cache_fox_logf.transpose(0, 1, 3, 2)
    dil_caches = [c.transpose(0, 1, 3, 4, 5, 2).reshape(c.shape[0], DEC_BATCH, 2, D_MODEL, c.shape[2])
                  for c in (cache_dil_w128, cache_dil_w512, cache_dil_w2048)]
    cos_p, sin_p = _rope_tables(jnp.arange(SEQ))
    cos_s, sin_s = _rope_tables(jnp.full((1,), PAST_LEN))

    x = jnp.concatenate([x_prompt.reshape(N_PROMPT, D_MODEL), x_sample.reshape(DEC_BATCH, D_MODEL),
                         jnp.zeros((NT - N_PROMPT - DEC_BATCH, D_MODEL), f32)], axis=0)

    fk_p, fk_s, fv_p, fv_s, fl_p, fl_s = [], [], [], [], [], []
    dil_p = [[] for _ in DIL_CONFIGS]
    dil_s = [[] for _ in DIL_CONFIGS]
    for i in range(DEPTH):
        j = i // 2
        if i % 2 == 0:
            w = fox_w_qkv[j].astype(bf16)
            wf = jnp.pad(fox_w_f[j], ((0, 0), (0, pad_lanes)))
            bfv = jnp.pad(fox_b_f[j], (0, pad_lanes))[None, :]
            q, ktb, vb, kt, vt, lft, ct, ccol = fox_proj_prompt(x, w, wf, bfv)
            o_main = fox_attn_prompt(q, ktb, vb, ccol, ct)
            qs, ks, vs, lfs = fox_proj_sample(x, w, wf, bfv)
            o_s = fox_decode(page_table, qs[:, None], ks[:, None], vs[:, None], lfs[:, None], ck, cv, cl, j)
            fk_p.append(kt.reshape(BATCH, N_HEADS, HEAD_DIM, SEQ).transpose(0, 3, 1, 2))
            fv_p.append(vt.reshape(BATCH, N_HEADS, HEAD_DIM, SEQ).transpose(0, 3, 1, 2))
            fl_p.append(lft.transpose(0, 2, 1))
            fk_s.append(ks.reshape(DEC_BATCH, 1, N_HEADS, HEAD_DIM))
            fv_s.append(vs.reshape(DEC_BATCH, 1, N_HEADS, HEAD_DIM))
            fl_s.append(lfs[:, :N_HEADS].reshape(DEC_BATCH, 1, N_HEADS))
            wo = fox_w_o[j].astype(bf16)
        else:
            w = dil_w_qkv[j].astype(bf16)
            os_, ls_ = [], []
            for gi, (win, _) in enumerate(DIL_CONFIGS):
                qr, kr, vr, kvt = dil_proj_prompt(x, w, cos_p, sin_p, gi)
                o_g, l_g = dil_attn_prompt(qr, kr, vr)
                os_.append(o_g)
                ls_.append(l_g)
                keep = min(win, SEQ)
                rows = kvt[:, :, :, SEQ - keep:].reshape(BATCH, 2, N_HEADS, HEAD_DIM, keep)
                dil_p[gi].append(rows.transpose(0, 4, 1, 2, 3))
            o_main = dil_merge_prompt(*os_, *ls_)
            qs, ks, vs = dil_proj_sample(x, w, cos_s, sin_s)
            o_s = dil_decode(qs.transpose(1, 0, 2), ks.transpose(1, 0, 2), vs.transpose(1, 0, 2),
                             *dil_caches, j)
            for gi in range(len(DIL_CONFIGS)):
                dil_s[gi].append(jnp.stack([ks[gi], vs[gi]], axis=1).reshape(DEC_BATCH, 1, 2, N_HEADS, HEAD_DIM))
            wo = dil_w_o[j].astype(bf16)

        o_tail = jnp.pad(o_s.reshape(DEC_BATCH, D_MODEL).astype(bf16), ((0, ROW_TILE - DEC_BATCH), (0, 0)))
        wr = jnp.pad(moe_w_router[i], ((0, 0), (0, LANES - N_EXPERTS)))
        br = jnp.concatenate([moe_b_router[i], jnp.full((LANES - N_EXPERTS,), NEG_INF, f32)])[None, :]
        x1, x1t, idx, gate, rank, cnt = post_attn(o_main, o_tail, x, wo, ln_g[i, 0][None, :], ln_b[i, 0][None, :], wr, br)
        ge, nv, nu, idx_arr = _route(idx, rank, cnt)
        y = moe_experts(ge, nv, nu, x1t, idx_arr, moe_w1, moe_b1[:, :, None, :], moe_w2, moe_b2[:, :, None, :], i)
        x = moe_combine(y, gate, x1, ln_g[i, 1][None, :], ln_b[i, 1][None, :])

    xp = x[:N_PROMPT].reshape(BATCH, SEQ, D_MODEL)
    xs = x[N_PROMPT:N_PROMPT + DEC_BATCH].reshape(DEC_BATCH, 1, D_MODEL)
    return (xp, xs,
            jnp.stack(fk_p), jnp.stack(fk_s), jnp.stack(fv_p), jnp.stack(fv_s), jnp.stack(fl_p), jnp.stack(fl_s),
            jnp.stack(dil_p[0]), jnp.stack(dil_s[0]), jnp.stack(dil_p[1]), jnp.stack(dil_s[1]),
            jnp.stack(dil_p[2]), jnp.stack(dil_s[2]))
```

```python
import functools

import jax
import jax.numpy as jnp
from jax import lax
from jax.experimental import pallas as pl
from jax.experimental.pallas import tpu as pltpu

f32, bf16, i32 = jnp.float32, jnp.bfloat16, jnp.int32

D_MODEL = 1024
BATCH = 8
SEQ = 2048
DEPTH = 4
DEC_BATCH = 32
PAST_LEN = 8192
PAGE_SIZE = 128
HEAD_DIM = 64
N_HEADS = 16
DIL_CONFIGS = ((128, 1), (512, 4), (2048, 16))
N_EXPERTS = 32
TOP_K = 4
SWIGLU_ALPHA = 1.702
SWIGLU_LIMIT = 7.0
ROPE_THETA = 10000.0
LN_EPS = 1e-5
DEEPNORM_ALPHA = (2 * DEPTH) ** 0.25
SCALE = HEAD_DIM ** -0.5
NEG_INF = -1e30

LANES = 128
N_PROMPT = BATCH * SEQ
ROW_TILE = 512
NT = N_PROMPT + ROW_TILE
N_ROW_TILES = NT // ROW_TILE
TILES_PER_SEQ = SEQ // ROW_TILE
MOE_M = 256
MOE_G = NT * TOP_K // MOE_M + N_EXPERTS
N_PAGES = PAST_LEN // PAGE_SIZE
PAGES_PER_STEP = 8
VMEM_LIMIT = 56 * 1024 * 1024

_HI = lax.Precision.HIGHEST


def _cp(sem, vmem=VMEM_LIMIT):
    return pltpu.CompilerParams(dimension_semantics=sem, vmem_limit_bytes=vmem)


def _dot(a, b, precision=None):
    return jnp.dot(a, b, preferred_element_type=f32, precision=precision)


def _dot_nt(a, b):
    return lax.dot_general(a, b, (((1,), (1,)), ((), ())), preferred_element_type=f32)


def _layer_norm(z, g, b):
    mu = jnp.mean(z, axis=-1, keepdims=True)
    zc = z - mu
    var = jnp.mean(zc * zc, axis=-1, keepdims=True)
    return zc * lax.rsqrt(var + LN_EPS) * g + b


def _log_sigmoid(z):
    return jnp.minimum(z, 0.0) - jnp.log1p(jnp.exp(-jnp.abs(z)))


def _fox_proj_kernel(x_ref, w_ref, wf_ref, bf_ref,
                     q_ref, ktb_ref, vb_ref, kt_ref, vt_ref, lft_ref, ct_ref, ccol_ref, carry_ref):
    i = pl.program_id(0)

    @pl.when(i % TILES_PER_SEQ == 0)
    def _():
        carry_ref[...] = jnp.zeros_like(carry_ref)

    x = x_ref[...]
    acc = _dot(x.astype(bf16), w_ref[...])
    q_ref[...] = (acc[:, :D_MODEL] * SCALE).astype(bf16)
    kt = acc[:, D_MODEL:2 * D_MODEL].T
    kt_ref[...] = kt
    ktb_ref[...] = kt.astype(bf16)
    v = acc[:, 2 * D_MODEL:]
    vt_ref[...] = v.T
    vb_ref[...] = v.astype(bf16)
    lf = _log_sigmoid(_dot(x, wf_ref[...], _HI) + bf_ref[...])
    tm = x.shape[0]
    r = lax.broadcasted_iota(i32, (tm, tm), 0)
    c = lax.broadcasted_iota(i32, (tm, tm), 1)
    cs = _dot((c <= r).astype(f32), lf, _HI) + carry_ref[...]
    carry_ref[...] = cs[tm - 1:tm, :]
    ccol_ref[...] = cs
    lft_ref[...] = lf.T[:N_HEADS, :]
    ct_ref[...] = cs.T[:N_HEADS, :]


def fox_proj_prompt(x, w, wf, bfv):
    tm = ROW_TILE
    n = N_PROMPT // tm
    seq_map = lambda i: (i // TILES_PER_SEQ, 0, i % TILES_PER_SEQ)
    row_map = lambda i: (i, 0)
    const = lambda i: (0, 0)
    return pl.pallas_call(
        _fox_proj_kernel,
        grid=(n,),
        in_specs=[pl.BlockSpec((tm, D_MODEL), row_map),
                  pl.BlockSpec((D_MODEL, 3 * D_MODEL), const),
                  pl.BlockSpec((D_MODEL, LANES), const),
                  pl.BlockSpec((1, LANES), const)],
        out_specs=[pl.BlockSpec((tm, D_MODEL), row_map),
                   pl.BlockSpec((None, D_MODEL, tm), seq_map),
                   pl.BlockSpec((tm, D_MODEL), row_map),
                   pl.BlockSpec((None, D_MODEL, tm), seq_map),
                   pl.BlockSpec((None, D_MODEL, tm), seq_map),
                   pl.BlockSpec((None, N_HEADS, tm), seq_map),
                   pl.BlockSpec((None, N_HEADS, tm), seq_map),
                   pl.BlockSpec((tm, LANES), row_map)],
        out_shape=[jax.ShapeDtypeStruct((N_PROMPT, D_MODEL), bf16),
                   jax.ShapeDtypeStruct((BATCH, D_MODEL, SEQ), bf16),
                   jax.ShapeDtypeStruct((N_PROMPT, D_MODEL), bf16),
                   jax.ShapeDtypeStruct((BATCH, D_MODEL, SEQ), f32),
                   jax.ShapeDtypeStruct((BATCH, D_MODEL, SEQ), f32),
                   jax.ShapeDtypeStruct((BATCH, N_HEADS, SEQ), f32),
                   jax.ShapeDtypeStruct((BATCH, N_HEADS, SEQ), f32),
                   jax.ShapeDtypeStruct((N_PROMPT, LANES), f32)],
        scratch_shapes=[pltpu.VMEM((1, LANES), f32)],
        compiler_params=_cp(("arbitrary",)),
        name="fox_proj_prompt",
    )(x, w, wf, bfv)


def _fox_proj_sample_kernel(x_ref, w_ref, wf_ref, bf_ref, q_ref, k_ref, v_ref, lf_ref):
    x = x_ref[...]
    acc = _dot(x.astype(bf16), w_ref[...])
    q_ref[...] = acc[:, :D_MODEL] * SCALE
    k_ref[...] = acc[:, D_MODEL:2 * D_MODEL]
    v_ref[...] = acc[:, 2 * D_MODEL:]
    lf_ref[...] = _log_sigmoid(_dot(x, wf_ref[...], _HI) + bf_ref[...])


def fox_proj_sample(x, w, wf, bfv):
    nb = DEC_BATCH
    const = lambda i: (0, 0)
    return pl.pallas_call(
        _fox_proj_sample_kernel,
        grid=(1,),
        in_specs=[pl.BlockSpec((nb, D_MODEL), lambda i: (N_PROMPT // nb, 0)),
                  pl.BlockSpec((D_MODEL, 3 * D_MODEL), const),
                  pl.BlockSpec((D_MODEL, LANES), const),
                  pl.BlockSpec((1, LANES), const)],
        out_specs=[pl.BlockSpec((nb, D_MODEL), const)] * 3 + [pl.BlockSpec((nb, LANES), const)],
        out_shape=[jax.ShapeDtypeStruct((nb, D_MODEL), f32)] * 3 + [jax.ShapeDtypeStruct((nb, LANES), f32)],
        compiler_params=_cp(("arbitrary",)),
        name="fox_proj_sample",
    )(x, w, wf, bfv)


def _fox_attn_kernel(q_ref, kt_ref, v_ref, ccol_ref, ct_ref, o_ref, m_ref, l_ref, acc_ref, *, bq, bk):
    qi = pl.program_id(1)
    ki = pl.program_id(2)

    @pl.when(ki == 0)
    def _():
        m_ref[...] = jnp.full_like(m_ref, NEG_INF)
        l_ref[...] = jnp.zeros_like(l_ref)
        acc_ref[...] = jnp.zeros_like(acc_ref)

    @pl.when(ki <= qi)
    def _():
        row = qi * bq + lax.broadcasted_iota(i32, (bq, bk), 0)
        col = ki * bk + lax.broadcasted_iota(i32, (bq, bk), 1)
        causal = col <= row
        lo = lax.broadcasted_iota(i32, (1, LANES), 1) < HEAD_DIM
        for hp in range(N_HEADS // 2):
            sl = slice(hp * LANES, (hp + 1) * LANES)
            q2 = q_ref[:, sl]
            kt2 = kt_ref[sl, :]
            v2 = v_ref[:, sl]
            pvs, alphas = [], []
            for par in range(2):
                h = 2 * hp + par
                msk = lo if par == 0 else jnp.logical_not(lo)
                s = _dot(jnp.where(msk, q2, jnp.zeros_like(q2)), kt2)
                s = s + ccol_ref[:, h:h + 1] - ct_ref[h:h + 1, :]
                s = jnp.where(causal, s, NEG_INF)
                m_prev = m_ref[:, h:h + 1]
                m_new = jnp.maximum(m_prev, jnp.max(s, axis=1, keepdims=True))
                alpha = jnp.exp(m_prev - m_new)
                p = jnp.exp(s - m_new)
                l_ref[:, h:h + 1] = alpha * l_ref[:, h:h + 1] + jnp.sum(p, axis=1, keepdims=True)
                m_ref[:, h:h + 1] = m_new
                pvs.append(_dot(p.astype(bf16), jnp.where(msk, v2, jnp.zeros_like(v2))))
                alphas.append(alpha)
            acc_ref[:, sl] = acc_ref[:, sl] * jnp.where(lo, alphas[0], alphas[1]) + pvs[0] + pvs[1]

    @pl.when(ki == qi)
    def _():
        lo = lax.broadcasted_iota(i32, (1, LANES), 1) < HEAD_DIM
        for hp in range(N_HEADS // 2):
            sl = slice(hp * LANES, (hp + 1) * LANES)
            linv = jnp.where(lo, 1.0 / l_ref[:, 2 * hp:2 * hp + 1], 1.0 / l_ref[:, 2 * hp + 1:2 * hp + 2])
            o_ref[:, sl] = (acc_ref[:, sl] * linv).astype(o_ref.dtype)


def fox_attn_prompt(q, kt, v, ccol, ct):
    bq = bk = ROW_TILE
    nq = SEQ // bq
    kmap = lambda b, qi, ki: (b, 0, jnp.minimum(ki, qi))
    return pl.pallas_call(
        functools.partial(_fox_attn_kernel, bq=bq, bk=bk),
        grid=(BATCH, nq, nq),
        in_specs=[pl.BlockSpec((bq, D_MODEL), lambda b, qi, ki: (b * nq + qi, 0)),
                  pl.BlockSpec((None, D_MODEL, bk), kmap),
                  pl.BlockSpec((bk, D_MODEL), lambda b, qi, ki: (b * nq + jnp.minimum(ki, qi), 0)),
                  pl.BlockSpec((bq, LANES), lambda b, qi, ki: (b * nq + qi, 0)),
                  pl.BlockSpec((None, N_HEADS, bk), kmap)],
        out_specs=pl.BlockSpec((bq, D_MODEL), lambda b, qi, ki: (b * nq + qi, 0)),
        out_shape=jax.ShapeDtypeStruct((N_PROMPT, D_MODEL), bf16),
        scratch_shapes=[pltpu.VMEM((bq, LANES), f32), pltpu.VMEM((bq, LANES), f32), pltpu.VMEM((bq, D_MODEL), f32)],
        compiler_params=_cp(("parallel", "arbitrary", "arbitrary")),
        name="fox_attn_prompt",
    )(q, kt, v, ccol, ct)


def _row_to_cols(row):
    return jnp.broadcast_to(row, (LANES, row.shape[1])).T


def _decode_chunk(kt_rows, vt_rows, qt_ref, bias, valid, m_ref, l_ref, acc_ref, tmp_ref, base=0):
    heads = slice(base // HEAD_DIM, base // HEAD_DIM + N_HEADS)
    for h in range(N_HEADS):
        rows = slice(base + h * HEAD_DIM, base + (h + 1) * HEAD_DIM)
        tmp_ref[0, h:h + 1, :] = jnp.sum(kt_rows(h) * qt_ref[rows, :], axis=0, keepdims=True)
    u = tmp_ref[0]
    if bias is not None:
        u = u - bias
    if valid is not None:
        u = jnp.where(valid, u, NEG_INF)
    m_prev = m_ref[heads, :]
    m_new = jnp.maximum(m_prev, u)
    alpha = jnp.exp(m_prev - m_new)
    p = jnp.exp(u - m_new)
    l_ref[heads, :] = alpha * l_ref[heads, :] + p
    m_ref[heads, :] = m_new
    tmp_ref[1] = alpha
    tmp_ref[2] = p
    for h in range(N_HEADS):
        rows = slice(base + h * HEAD_DIM, base + (h + 1) * HEAD_DIM)
        acc_ref[rows, :] = acc_ref[rows, :] * tmp_ref[1, h:h + 1, :] + vt_rows(h) * tmp_ref[2, h:h + 1, :]


def _decode_finish(h, qt_ref, knt, vnt, shift, m_ref, l_ref, acc_ref, base=0):
    rows = slice(base + h * HEAD_DIM, base + (h + 1) * HEAD_DIM)
    hrow = slice(base // HEAD_DIM + h, base // HEAD_DIM + h + 1)
    m_vec = m_ref[hrow, :]
    m_past = jnp.max(m_vec, axis=1, keepdims=True)
    sc = jnp.exp(m_vec - m_past)
    l_past = jnp.sum(l_ref[hrow, :] * sc, axis=1, keepdims=True)
    o_past = jnp.sum(acc_ref[rows, :] * sc, axis=1, keepdims=True)
    s_new = jnp.sum(qt_ref[rows, :] * knt[h * HEAD_DIM:(h + 1) * HEAD_DIM, :], axis=0, keepdims=True)
    m_sh = m_past + shift
    m_f = jnp.maximum(m_sh, s_new)
    a = jnp.exp(m_sh - m_f)
    bn = jnp.exp(s_new - m_f)
    l_f = l_past * a + bn
    o = (o_past * a + bn * vnt[h * HEAD_DIM:(h + 1) * HEAD_DIM, :]) / l_f
    return o, m_f + jnp.log(l_f)


def _fox_decode_kernel(pt_ref, q_ref, kn_ref, vn_ref, lfn_ref, *rest):
    np_ = PAGES_PER_STEP
    k_refs, v_refs, lf_refs = rest[:np_], rest[np_:2 * np_], rest[2 * np_:3 * np_]
    o_ref, qt_ref, m_ref, l_ref, acc_ref, carry_ref, ocol_ref, tmp_ref = rest[3 * np_:]
    s = pl.program_id(1)

    def head_rows(ref):
        return lambda h: ref[h * HEAD_DIM:(h + 1) * HEAD_DIM, :]

    @pl.when(s == 0)
    def _():
        qt_ref[...] = _row_to_cols(q_ref[...])
        m_ref[...] = jnp.full_like(m_ref, NEG_INF)
        l_ref[...] = jnp.zeros_like(l_ref)
        acc_ref[...] = jnp.zeros_like(acc_ref)
        carry_ref[...] = jnp.zeros_like(carry_ref)

    r = lax.broadcasted_iota(i32, (LANES, LANES), 0)
    c = lax.broadcasted_iota(i32, (LANES, LANES), 1)
    tri = (r <= c).astype(f32)
    for t in range(np_):
        lf = lf_refs[t][...]
        pre = _dot(lf, tri, _HI) + carry_ref[...]
        carry_ref[...] = carry_ref[...] + jnp.sum(lf, axis=1, keepdims=True)
        _decode_chunk(head_rows(k_refs[t]), head_rows(v_refs[t]), qt_ref, pre, None, m_ref, l_ref, acc_ref, tmp_ref)

    @pl.when(s == pl.num_programs(1) - 1)
    def _():
        knt = _row_to_cols(kn_ref[...])
        vnt = _row_to_cols(vn_ref[...])
        for h in range(N_HEADS):
            shift = carry_ref[h:h + 1, :] + lfn_ref[:, h:h + 1]
            o, _ = _decode_finish(h, qt_ref, knt, vnt, shift, m_ref, l_ref, acc_ref)
            ocol_ref[h * HEAD_DIM:(h + 1) * HEAD_DIM, :] = jnp.broadcast_to(o, (HEAD_DIM, LANES))
        o_ref[...] = ocol_ref[...].T[0:1, :]


def fox_decode(page_table, q, kn, vn, lfn, ck, cv, cl, layer):
    np_ = PAGES_PER_STEP
    row = lambda b, s, pt: (b, 0, 0)

    def page_map(t):
        return lambda b, s, pt: (layer, pt[b, s * np_ + t], 0, 0)

    kv_specs = [pl.BlockSpec((None, None, D_MODEL, PAGE_SIZE), page_map(t)) for t in range(np_)]
    lf_specs = [pl.BlockSpec((None, None, N_HEADS, PAGE_SIZE), page_map(t)) for t in range(np_)]
    grid_spec = pltpu.PrefetchScalarGridSpec(
        num_scalar_prefetch=1,
        grid=(DEC_BATCH, N_PAGES // np_),
        in_specs=[pl.BlockSpec((None, 1, D_MODEL), row)] * 3 + [pl.BlockSpec((None, 1, LANES), row)]
                 + kv_specs + kv_specs + lf_specs,
        out_specs=pl.BlockSpec((None, 1, D_MODEL), row),
        scratch_shapes=[pltpu.VMEM((D_MODEL, LANES), f32),
                        pltpu.VMEM((N_HEADS, LANES), f32), pltpu.VMEM((N_HEADS, LANES), f32),
                        pltpu.VMEM((D_MODEL, LANES), f32), pltpu.VMEM((N_HEADS, LANES), f32),
                        pltpu.VMEM((D_MODEL, LANES), f32), pltpu.VMEM((3, N_HEADS, LANES), f32)])
    return pl.pallas_call(
        _fox_decode_kernel,
        grid_spec=grid_spec,
        out_shape=jax.ShapeDtypeStruct((DEC_BATCH, 1, D_MODEL), f32),
        compiler_params=_cp(("parallel", "arbitrary")),
        name="fox_decode",
    )(page_table, q, kn, vn, lfn, *([ck] * np_), *([cv] * np_), *([cl] * np_))


def _post_attn_kernel(o_ref, ot_ref, x_ref, wo_ref, g_ref, b_ref, wr_ref, br_ref,
                      x1_ref, x1t_ref, idx_ref, gate_ref, rank_ref, cnt_ref, carry_ref, *, n_main):
    i = pl.program_id(0)

    @pl.when(i == 0)
    def _():
        carry_ref[...] = jnp.zeros_like(carry_ref)

    o = jnp.where(i < n_main, o_ref[...], ot_ref[...])
    x1 = _layer_norm(DEEPNORM_ALPHA * x_ref[...] + _dot(o, wo_ref[...]), g_ref[...], b_ref[...])
    x1_ref[...] = x1
    nblk = D_MODEL // LANES
    for c in range(nblk):
        x1t_ref[pl.ds(c, x1.shape[0], stride=nblk), :] = x1[:, c * LANES:(c + 1) * LANES]

    tm = x1.shape[0]
    logits = _dot(x1, wr_ref[...], _HI) + br_ref[...]
    lane = lax.broadcasted_iota(i32, (tm, LANES), 1)
    lane_f = lane.astype(f32)
    lg = logits
    onehots, vals, idxs = [], [], []
    for _ in range(TOP_K):
        mx = jnp.max(lg, axis=1, keepdims=True)
        ix = jnp.min(jnp.where(lg == mx, lane_f, float(LANES)), axis=1, keepdims=True)
        oh = lane_f == ix
        lg = jnp.where(oh, -3e38, lg)
        onehots.append(oh)
        vals.append(mx)
        idxs.append(ix)
    es = [jnp.exp(v - vals[0]) for v in vals]
    den = es[0] + es[1] + es[2] + es[3]
    chosen = jnp.zeros((tm, LANES), f32)
    for oh in onehots:
        chosen = chosen + oh.astype(f32)
    r = lax.broadcasted_iota(i32, (tm, tm), 0)
    c = lax.broadcasted_iota(i32, (tm, tm), 1)
    before = _dot((c < r).astype(bf16), chosen.astype(bf16)) + carry_ref[...]
    carry_ref[...] = carry_ref[...] + jnp.sum(chosen, axis=0, keepdims=True)
    cnt_ref[...] = carry_ref[...]
    idx_o = jnp.zeros((tm, LANES), f32)
    gate_o = jnp.zeros((tm, LANES), f32)
    rank_o = jnp.zeros((tm, LANES), f32)
    for k in range(TOP_K):
        sel = lane == k
        rk = jnp.sum(jnp.where(onehots[k], before, 0.0), axis=1, keepdims=True)
        idx_o = jnp.where(sel, idxs[k], idx_o)
        gate_o = jnp.where(sel, es[k] / den, gate_o)
        rank_o = jnp.where(sel, rk, rank_o)
    idx_ref[...] = idx_o.astype(i32)
    gate_ref[...] = gate_o
    rank_ref[...] = rank_o.astype(i32)


def post_attn(o_main, o_tail, x, wo, g, b, wr, br):
    tm = ROW_TILE
    n_main = N_PROMPT // tm
    row = lambda i: (i, 0)
    const = lambda i: (0, 0)
    return pl.pallas_call(
        functools.partial(_post_attn_kernel, n_main=n_main),
        grid=(N_ROW_TILES,),
        in_specs=[pl.BlockSpec((tm, D_MODEL), lambda i: (jnp.minimum(i, n_main - 1), 0)),
                  pl.BlockSpec((tm, D_MODEL), const),
                  pl.BlockSpec((tm, D_MODEL), row),
                  pl.BlockSpec((D_MODEL, D_MODEL), const),
                  pl.BlockSpec((1, D_MODEL), const),
                  pl.BlockSpec((1, D_MODEL), const),
                  pl.BlockSpec((D_MODEL, LANES), const),
                  pl.BlockSpec((1, LANES), const)],
        out_specs=[pl.BlockSpec((tm, D_MODEL), row),
                   pl.BlockSpec((tm * D_MODEL // LANES, LANES), row),
                   pl.BlockSpec((tm, LANES), row),
                   pl.BlockSpec((tm, LANES), row),
                   pl.BlockSpec((tm, LANES), row),
                   pl.BlockSpec((1, LANES), const)],
        out_shape=[jax.ShapeDtypeStruct((NT, D_MODEL), f32),
                   jax.ShapeDtypeStruct((NT * D_MODEL // LANES, LANES), f32),
                   jax.ShapeDtypeStruct((NT, LANES), i32),
                   jax.ShapeDtypeStruct((NT, LANES), f32),
                   jax.ShapeDtypeStruct((NT, LANES), i32),
                   jax.ShapeDtypeStruct((1, LANES), f32)],
        scratch_shapes=[pltpu.VMEM((1, LANES), f32)],
        compiler_params=_cp(("arbitrary",)),
        name="post_attn",
    )(o_main, o_tail, x, wo, g, b, wr, br)


def _moe_kernel(ge_ref, nv_ref, nu_ref, x_hbm, idx_hbm, w1_ref, b1_ref, w2_ref, b2_ref, y_hbm,
                xbuf, ybuf, idx_smem, w1b, w2b, gsem, ssem, isem):
    g = pl.program_id(0)
    nu = nu_ref[0]
    m = MOE_M
    nblk = D_MODEL // LANES

    def idx_copy(gg, sl):
        return pltpu.make_async_copy(idx_hbm.at[gg], idx_smem.at[sl], isem.at[sl])

    def gather_row(sl, j):
        src = pl.multiple_of(idx_smem[sl, 0, j], 8)
        return pltpu.make_async_copy(x_hbm.at[pl.ds(src, 8)], xbuf.at[sl, pl.ds(j * 8, 8)], gsem.at[sl])

    def scatter_row(sl, j):
        dst = pl.multiple_of(idx_smem[sl, 1, j], 8)
        return pltpu.make_async_copy(ybuf.at[sl, pl.ds(j * 8, 8)], y_hbm.at[pl.ds(dst, 8)], ssem.at[sl])

    def start_rows(row_copy, sl, n):
        @pl.when(n == m)
        def _():
            for j in range(m):
                row_copy(sl, j).start(priority=j % 2)

        @pl.when(n < m)
        def _():
            def body(j, carry):
                row_copy(sl, j).start()
                return carry
            lax.fori_loop(0, n, body, 0)

    def wait_rows(n, wait_static):
        @pl.when(n == m)
        def _():
            wait_static(m)

        @pl.when(n < m)
        def _():
            bit = m // 2
            while bit >= 1:
                @pl.when((n & bit) != 0)
                def _(bit=bit):
                    wait_static(bit)
                bit //= 2

    def wait_gathers(sl, n):
        wait_rows(n, lambda k: pltpu.make_async_copy(
            x_hbm.at[pl.ds(0, 8 * k)], xbuf.at[sl, pl.ds(0, 8 * k)], gsem.at[sl]).wait())

    def wait_scatters(sl, n):
        wait_rows(n, lambda k: pltpu.make_async_copy(
            ybuf.at[sl, pl.ds(0, 8 * k)], y_hbm.at[pl.ds(0, 8 * k)], ssem.at[sl]).wait())

    def step(slot):
        other = 1 - slot

        @pl.when(g + 1 < nu)
        def _():
            idx_copy(g + 1, other).wait()
            start_rows(gather_row, other, nv_ref[g + 1])

        @pl.when(g >= 2)
        def _():
            wait_scatters(slot, nv_ref[jnp.maximum(g - 2, 0)])

        wait_gathers(slot, nv_ref[g])

        @pl.when(jnp.logical_or(g == 0, ge_ref[g] != ge_ref[jnp.maximum(g - 1, 0)]))
        def _():
            w1b[...] = w1_ref[...].astype(bf16)
            w2b[...] = w2_ref[...].astype(bf16)

        x = jnp.concatenate([xbuf[slot, pl.ds(c, m, stride=nblk), :] for c in range(nblk)], axis=1).astype(bf16)
        h = _dot(x, w1b[...]) + b1_ref[...]
        glu = jnp.minimum(h[:, :D_MODEL], SWIGLU_LIMIT)
        lin = jnp.clip(h[:, D_MODEL:], -SWIGLU_LIMIT, SWIGLU_LIMIT)
        act = glu * (1.0 / (1.0 + jnp.exp(-SWIGLU_ALPHA * glu))) * (lin + 1.0)
        y = _dot(act.astype(bf16), w2b[...]) + b2_ref[...]
        for c in range(nblk):
            ybuf[slot, pl.ds(c, m, stride=nblk), :] = y[:, c * LANES:(c + 1) * LANES]

        start_rows(scatter_row, slot, nv_ref[g])

        @pl.when(g + 2 < nu)
        def _():
            idx_copy(g + 2, slot).start()

        @pl.when(g == nu - 1)
        def _():
            wait_scatters(slot, nv_ref[g])

            @pl.when(g >= 1)
            def _():
                wait_scatters(other, nv_ref[jnp.maximum(g - 1, 0)])

    @pl.when(g < nu)
    def _():
        @pl.when(g == 0)
        def _():
            xbuf[...] = jnp.zeros_like(xbuf)
            first = idx_copy(0, 0)
            first.start()
            first.wait()
            start_rows(gather_row, 0, nv_ref[0])

            @pl.when(nu > 1)
            def _():
                idx_copy(1, 1).start()

        @pl.when(g % 2 == 0)
        def _():
            step(0)

        @pl.when(g % 2 == 1)
        def _():
            step(1)


def moe_experts(ge, nv, nu, x, idx, w1, b1, w2, b2, layer):
    m = MOE_M
    wmap = lambda g, ge, nv, nu: (layer, ge[g], 0, 0)
    grid_spec = pltpu.PrefetchScalarGridSpec(
        num_scalar_prefetch=3,
        grid=(MOE_G,),
        in_specs=[pl.BlockSpec(memory_space=pl.ANY),
                  pl.BlockSpec(memory_space=pl.ANY),
                  pl.BlockSpec((None, None, D_MODEL, 2 * D_MODEL), wmap),
                  pl.BlockSpec((None, None, 1, 2 * D_MODEL), wmap),
                  pl.BlockSpec((None, None, D_MODEL, D_MODEL), wmap),
                  pl.BlockSpec((None, None, 1, D_MODEL), wmap)],
        out_specs=pl.BlockSpec(memory_space=pl.ANY),
        scratch_shapes=[pltpu.VMEM((2, m * D_MODEL // LANES, LANES), f32),
                        pltpu.VMEM((2, m * D_MODEL // LANES, LANES), f32),
                        pltpu.SMEM((2, 2, m), i32),
                        pltpu.VMEM((D_MODEL, 2 * D_MODEL), bf16),
                        pltpu.VMEM((D_MODEL, D_MODEL), bf16),
                        pltpu.SemaphoreType.DMA((2,)),
                        pltpu.SemaphoreType.DMA((2,)),
                        pltpu.SemaphoreType.DMA((2,))])
    return pl.pallas_call(
        _moe_kernel,
        grid_spec=grid_spec,
        out_shape=jax.ShapeDtypeStruct((TOP_K * NT * D_MODEL // LANES, LANES), f32),
        compiler_params=_cp(("arbitrary",)),
        name="moe_experts",
    )(ge, nv, nu, x, idx, w1, b1, w2, b2)


def _route(idx, rank, cnt):
    m, g = MOE_M, MOE_G
    idx4 = idx[:, :TOP_K]
    rank4 = rank[:, :TOP_K]
    counts = cnt[0, :N_EXPERTS].astype(i32)
    padded = (counts + m - 1) // m * m
    pad_end = jnp.cumsum(padded)
    pad_start = pad_end - padded
    dest = pad_start[idx4] + rank4
    assign = jnp.arange(NT, dtype=i32)[:, None] + NT * jnp.arange(TOP_K, dtype=i32)[None, :]
    rows = jnp.zeros((g * m,), i32).at[dest.reshape(-1)].set(assign.reshape(-1), unique_indices=True)
    per_tok = D_MODEL // LANES
    idx_arr = jnp.stack([rows % NT * per_tok, rows * per_tok], axis=0).reshape(2, g, m).transpose(1, 0, 2)
    gstart = jnp.arange(g, dtype=i32) * m
    ge = jnp.minimum(jnp.sum((gstart[:, None] >= pad_end[None, :]).astype(i32), axis=1), N_EXPERTS - 1)
    nv = jnp.clip(counts[ge] - (gstart - pad_start[ge]), 0, m).astype(i32)
    nu = (pad_end[-1:] // m).astype(i32)
    return ge, nv, nu, idx_arr


def _moe_combine_kernel(y0, y1, y2, y3, gate_ref, x_ref, g_ref, b_ref, o_ref):
    gt = gate_ref[...]
    cols = []
    nblk = D_MODEL // LANES
    tm = x_ref.shape[0]
    for c in range(nblk):
        rows = pl.ds(c, tm, stride=nblk)
        cols.append(gt[:, 0:1] * y0[rows, :] + gt[:, 1:2] * y1[rows, :] + gt[:, 2:3] * y2[rows, :] + gt[:, 3:4] * y3[rows, :])
    y = jnp.concatenate(cols, axis=1)
    o_ref[...] = _layer_norm(DEEPNORM_ALPHA * x_ref[...] + y, g_ref[...], b_ref[...])


def moe_combine(y, gate, x, g, b):
    tm = ROW_TILE
    row = lambda i: (i, 0)
    const = lambda i: (0, 0)
    y_specs = [pl.BlockSpec((tm * D_MODEL // LANES, LANES), (lambda i, k=k: (k * N_ROW_TILES + i, 0))) for k in range(TOP_K)]
    return pl.pallas_call(
        _moe_combine_kernel,
        grid=(N_ROW_TILES,),
        in_specs=y_specs + [pl.BlockSpec((tm, LANES), row), pl.BlockSpec((tm, D_MODEL), row),
                            pl.BlockSpec((1, D_MODEL), const), pl.BlockSpec((1, D_MODEL), const)],
        out_specs=pl.BlockSpec((tm, D_MODEL), row),
        out_shape=jax.ShapeDtypeStruct((NT, D_MODEL), f32),
        compiler_params=_cp(("parallel",)),
        name="moe_combine",
    )(y, y, y, y, gate, x, g, b)


def _rope(xb, cos, sin, lo32):
    partner = jnp.where(lo32, pltpu.roll(xb, LANES - HEAD_DIM // 2, 1), pltpu.roll(xb, HEAD_DIM // 2, 1))
    return xb * cos + partner * sin


def _rope_tables(pos):
    half = HEAD_DIM // 2
    lane = jnp.arange(LANES)
    inv = ROPE_THETA ** (-(lane % half).astype(f32) / half)
    ang = pos.astype(f32)[:, None] * inv[None, :]
    sign = jnp.where((lane % HEAD_DIM) < half, -1.0, 1.0).astype(f32)
    return jnp.cos(ang), jnp.sin(ang) * sign[None, :]


def _dil_proj_kernel(x_ref, w_ref, cos_ref, sin_ref, q_ref, k_ref, v_ref, kvt_ref, *scratch, dil):
    tm = x_ref.shape[0]
    nblk = D_MODEL // LANES
    acc = _dot(x_ref[...].astype(bf16), w_ref[...])
    cos, sin = cos_ref[...], sin_ref[...]
    lo32 = (lax.broadcasted_iota(i32, (1, LANES), 1) % HEAD_DIM) < HEAD_DIM // 2
    pieces = []
    for cb in range(3 * nblk):
        blk = acc[:, cb * LANES:(cb + 1) * LANES]
        if cb < nblk:
            blk = _rope(blk * SCALE, cos, sin, lo32)
        elif cb < 2 * nblk:
            blk = _rope(blk, cos, sin, lo32)
        pieces.append(blk)
    kvt_ref[0] = jnp.concatenate(pieces[nblk:2 * nblk], axis=1).T
    kvt_ref[1] = jnp.concatenate(pieces[2 * nblk:], axis=1).T
    outs = (q_ref, k_ref, v_ref)
    if dil == 1:
        for cb in range(3 * nblk):
            outs[cb // nblk][0, :, (cb % nblk) * LANES:(cb % nblk + 1) * LANES] = pieces[cb].astype(bf16)
    else:
        slab = scratch[0]
        for cb in range(3 * nblk):
            slab[cb] = pieces[cb]
        n = tm // dil
        for cb in range(3 * nblk):
            for r in range(dil):
                outs[cb // nblk][r, :, (cb % nblk) * LANES:(cb % nblk + 1) * LANES] = (
                    slab[cb, pl.ds(r, n, stride=dil), :].astype(bf16))


def dil_proj_prompt(x, w, cos, sin, group):
    dil = DIL_CONFIGS[group][1]
    tm = ROW_TILE
    lu = SEQ // dil
    n = N_PROMPT // tm
    res_map = lambda i: (i // TILES_PER_SEQ, 0, i % TILES_PER_SEQ, 0)
    res_spec = pl.BlockSpec((None, dil, tm // dil, D_MODEL), res_map)
    res_shape = jax.ShapeDtypeStruct((BATCH, dil, lu, D_MODEL), bf16)
    return pl.pallas_call(
        functools.partial(_dil_proj_kernel, dil=dil),
        grid=(n,),
        in_specs=[pl.BlockSpec((tm, D_MODEL), lambda i: (i, 0)),
                  pl.BlockSpec((D_MODEL, 3 * D_MODEL), lambda i: (0, group)),
                  pl.BlockSpec((tm, LANES), lambda i: (i % TILES_PER_SEQ, 0)),
                  pl.BlockSpec((tm, LANES), lambda i: (i % TILES_PER_SEQ, 0))],
        out_specs=[res_spec, res_spec, res_spec,
                   pl.BlockSpec((None, 2, D_MODEL, tm), lambda i: (i // TILES_PER_SEQ, 0, 0, i % TILES_PER_SEQ))],
        out_shape=[res_shape, res_shape, res_shape, jax.ShapeDtypeStruct((BATCH, 2, D_MODEL, SEQ), f32)],
        scratch_shapes=[] if dil == 1 else [pltpu.VMEM((3 * D_MODEL // LANES, tm, LANES), f32)],
        compiler_params=_cp(("parallel",)),
        name=f"dil_proj_prompt_g{group}",
    )(x, w, cos, sin)


def _dil_proj_sample_kernel(x_ref, w_ref, cos_ref, sin_ref, q_ref, k_ref, v_ref):
    nblk = D_MODEL // LANES
    acc = _dot(x_ref[...].astype(bf16), w_ref[...])
    cos, sin = cos_ref[...], sin_ref[...]
    lo32 = (lax.broadcasted_iota(i32, (1, LANES), 1) % HEAD_DIM) < HEAD_DIM // 2
    for gi in range(len(DIL_CONFIGS)):
        for cb in range(3 * nblk):
            col = gi * 3 * D_MODEL + cb * LANES
            blk = acc[:, col:col + LANES]
            dst = slice((cb % nblk) * LANES, (cb % nblk + 1) * LANES)
            if cb < nblk:
                q_ref[gi, :, dst] = _rope(blk * SCALE, cos, sin, lo32)
            elif cb < 2 * nblk:
                k_ref[gi, :, dst] = _rope(blk, cos, sin, lo32)
            else:
                v_ref[gi, :, dst] = blk


def dil_proj_sample(x, w, cos, sin):
    nb = DEC_BATCH
    ng = len(DIL_CONFIGS)
    const = lambda i: (0, 0)
    out_spec = pl.BlockSpec((ng, nb, D_MODEL), lambda i: (0, 0, 0))
    out_shape = jax.ShapeDtypeStruct((ng, nb, D_MODEL), f32)
    return pl.pallas_call(
        _dil_proj_sample_kernel,
        grid=(1,),
        in_specs=[pl.BlockSpec((nb, D_MODEL), lambda i: (N_PROMPT // nb, 0)),
                  pl.BlockSpec((D_MODEL, ng * 3 * D_MODEL), const),
                  pl.BlockSpec((1, LANES), const),
                  pl.BlockSpec((1, LANES), const)],
        out_specs=[out_spec] * 3,
        out_shape=[out_shape] * 3,
        compiler_params=_cp(("arbitrary",)),
        name="dil_proj_sample",
    )(x, w, cos, sin)


def _dil_attn_kernel(q_ref, kc_ref, kp_ref, vc_ref, vp_ref, o_ref, lse_ref, s_ref, p_ref):
    n = pl.program_id(2)
    blk = q_ref.shape[0]
    lane = lax.broadcasted_iota(i32, (1, LANES), 1)
    lo = lane < HEAD_DIM
    halves = (lo, jnp.logical_not(lo))
    for hp in range(N_HEADS // 2):
        sl = slice(hp * LANES, (hp + 1) * LANES)
        q2 = q_ref[:, sl]
        kk = jnp.concatenate([kp_ref[:, sl], kc_ref[:, sl]], axis=0)
        for par in range(2):
            h = 2 * hp + par
            s_ref[h * blk:(h + 1) * blk, :] = _dot_nt(jnp.where(halves[par], q2, jnp.zeros_like(q2)), kk)

    shape = (N_HEADS * blk, 2 * blk)
    i = lax.broadcasted_iota(i32, shape, 0) % blk
    j = lax.broadcasted_iota(i32, shape, 1)
    ok = jnp.logical_or(jnp.logical_and(jnp.logical_and(j < blk, j >= i), n > 0),
                        jnp.logical_and(j >= blk, j - blk <= i))
    s = jnp.where(ok, s_ref[...], NEG_INF)
    mx = jnp.max(s, axis=1, keepdims=True)
    p = jnp.exp(s - mx)
    l = jnp.sum(p, axis=1, keepdims=True)
    p_ref[...] = p.astype(bf16)
    linv = 1.0 / l
    lse_col = mx + jnp.log(l)

    lse = jnp.zeros((blk, LANES), f32)
    for hp in range(N_HEADS // 2):
        sl = slice(hp * LANES, (hp + 1) * LANES)
        vv = jnp.concatenate([vp_ref[:, sl], vc_ref[:, sl]], axis=0)
        o2 = jnp.zeros((blk, LANES), f32)
        for par in range(2):
            h = 2 * hp + par
            rows = slice(h * blk, (h + 1) * blk)
            o2 = o2 + _dot(p_ref[rows, :], jnp.where(halves[par], vv, jnp.zeros_like(vv))) * linv[rows, :]
            lse = jnp.where(lane == h, lse_col[rows, :], lse)
        o_ref[:, sl] = o2
    lse_ref[...] = lse


def dil_attn_prompt(q, k, v):
    _, dil, lu, _ = q.shape
    blk = 128
    nb = lu // blk
    cur = lambda b, r, n: (b, r, n, 0)
    prev = lambda b, r, n: (b, r, jnp.maximum(n - 1, 0), 0)
    spec = lambda m: pl.BlockSpec((None, None, blk, D_MODEL), m)
    return pl.pallas_call(
        _dil_attn_kernel,
        grid=(BATCH, dil, nb),
        in_specs=[spec(cur), spec(cur), spec(prev), spec(cur), spec(prev)],
        out_specs=[spec(cur), pl.BlockSpec((None, None, blk, LANES), cur)],
        out_shape=[jax.ShapeDtypeStruct((BATCH, dil, lu, D_MODEL), f32),
                   jax.ShapeDtypeStruct((BATCH, dil, lu, LANES), f32)],
        scratch_shapes=[pltpu.VMEM((N_HEADS * blk, 2 * blk), f32), pltpu.VMEM((N_HEADS * blk, 2 * blk), bf16)],
        compiler_params=_cp(("parallel", "parallel", "arbitrary")),
        name=f"dil_attn_prompt_d{dil}",
    )(q, k, k, v, v)


def _expand_heads(w):
    hrow = lax.broadcasted_iota(i32, (LANES, D_MODEL), 0)
    col = lax.broadcasted_iota(i32, (LANES, D_MODEL), 1)
    e = (col // HEAD_DIM == hrow).astype(bf16)
    hi = w.astype(bf16)
    lo = (w - hi.astype(f32)).astype(bf16)
    return _dot(hi, e) + _dot(lo, e)


def _dil_merge_kernel(o1_ref, o2_ref, o3_ref, l1_ref, l2_ref, l3_ref, out_ref, nat_o, nat_l):
    tm = out_ref.shape[0]
    nblk = D_MODEL // LANES
    for gi, (o_ref, l_ref) in enumerate(((o2_ref, l2_ref), (o3_ref, l3_ref))):
        dil = DIL_CONFIGS[gi + 1][1]
        n = tm // dil
        for r in range(dil):
            nat_l[gi, pl.ds(r, n, stride=dil), :] = l_ref[r]
            for cb in range(nblk):
                nat_o[gi, cb, pl.ds(r, n, stride=dil), :] = o_ref[r, :, cb * LANES:(cb + 1) * LANES]
    lses = (l1_ref[0], nat_l[0], nat_l[1])
    mx = jnp.maximum(jnp.maximum(lses[0], lses[1]), lses[2])
    es = [jnp.exp(l - mx) for l in lses]
    den = es[0] + es[1] + es[2]
    ws = [_expand_heads(e / den) for e in es]
    for cb in range(nblk):
        sl = slice(cb * LANES, (cb + 1) * LANES)
        out_ref[:, sl] = (ws[0][:, sl] * o1_ref[0, :, sl] + ws[1][:, sl] * nat_o[0, cb]
                          + ws[2][:, sl] * nat_o[1, cb]).astype(out_ref.dtype)


def dil_merge_prompt(o1, o2, o3, l1, l2, l3):
    tm = ROW_TILE
    res_map = lambda b, t: (b, 0, t, 0)

    def ospec(dil, width):
        return pl.BlockSpec((None, dil, tm // dil, width), res_map)

    dils = [d for _, d in DIL_CONFIGS]
    return pl.pallas_call(
        _dil_merge_kernel,
        grid=(BATCH, TILES_PER_SEQ),
        in_specs=[ospec(d, D_MODEL) for d in dils] + [ospec(d, LANES) for d in dils],
        out_specs=pl.BlockSpec((tm, D_MODEL), lambda b, t: (b * TILES_PER_SEQ + t, 0)),
        out_shape=jax.ShapeDtypeStruct((N_PROMPT, D_MODEL), bf16),
        scratch_shapes=[pltpu.VMEM((2, D_MODEL // LANES, tm, LANES), f32), pltpu.VMEM((2, tm, LANES), f32)],
        compiler_params=_cp(("parallel", "parallel")),
        name="dil_merge_prompt",
    )(o1, o2, o3, l1, l2, l3)


def _dil_decode_kernel(q_ref, kn_ref, vn_ref, c1_ref, c2_ref, c3_ref, o_ref,
                       qt_ref, m_ref, l_ref, acc_ref, ocol_ref, tmp_ref):
    c = pl.program_id(1)
    ng = len(DIL_CONFIGS)
    lane = lax.broadcasted_iota(i32, (1, LANES), 1)

    def group_chunks(gi, cache_ref, n_chunks, pos0):
        dil = DIL_CONFIGS[gi][1]
        for t in range(n_chunks):
            valid = None if dil == 1 else ((pos0 + t * LANES + lane) % dil) == 0

            def slab(kv, t=t):
                return lambda h: cache_ref[kv, h * HEAD_DIM:(h + 1) * HEAD_DIM, t * LANES:(t + 1) * LANES]

            _decode_chunk(slab(0), slab(1), qt_ref, None, valid, m_ref, l_ref, acc_ref, tmp_ref, base=gi * D_MODEL)

    @pl.when(c == 0)
    def _():
        for gi in range(ng):
            qt_ref[gi * D_MODEL:(gi + 1) * D_MODEL, :] = _row_to_cols(q_ref[gi:gi + 1, :])
        m_ref[...] = jnp.full_like(m_ref, NEG_INF)
        l_ref[...] = jnp.zeros_like(l_ref)
        acc_ref[...] = jnp.zeros_like(acc_ref)
        group_chunks(0, c1_ref, c1_ref.shape[2] // LANES, 0)
        group_chunks(1, c2_ref, c2_ref.shape[2] // LANES, 0)

    group_chunks(2, c3_ref, c3_ref.shape[2] // LANES, c * c3_ref.shape[2])

    @pl.when(c == pl.num_programs(1) - 1)
    def _():
        knts = [_row_to_cols(kn_ref[gi:gi + 1, :]) for gi in range(ng)]
        vnts = [_row_to_cols(vn_ref[gi:gi + 1, :]) for gi in range(ng)]
        for h in range(N_HEADS):
            outs = [_decode_finish(h, qt_ref, knts[gi], vnts[gi], 0.0, m_ref, l_ref, acc_ref, base=gi * D_MODEL)
                    for gi in range(ng)]
            mx = jnp.maximum(jnp.maximum(outs[0][1], outs[1][1]), outs[2][1])
            es = [jnp.exp(o[1] - mx) for o in outs]
            den = es[0] + es[1] + es[2]
            merged = (es[0] * outs[0][0] + es[1] * outs[1][0] + es[2] * outs[2][0]) / den
            ocol_ref[h * HEAD_DIM:(h + 1) * HEAD_DIM, :] = jnp.broadcast_to(merged, (HEAD_DIM, LANES))
        o_ref[...] = ocol_ref[...].T[0:1, :]


def dil_decode(q, kn, vn, c1, c2, c3, layer):
    ng = len(DIL_CONFIGS)
    chunk = 512
    n_chunks = c3.shape[-1] // chunk
    row = lambda b, c: (b, 0, 0)
    whole = lambda b, c: (layer, b, 0, 0, 0)
    return pl.pallas_call(
        _dil_decode_kernel,
        grid=(DEC_BATCH, n_chunks),
        in_specs=[pl.BlockSpec((None, ng, D_MODEL), row)] * 3
                 + [pl.BlockSpec((None, None, 2, D_MODEL, c1.shape[-1]), whole),
                    pl.BlockSpec((None, None, 2, D_MODEL, c2.shape[-1]), whole),
                    pl.BlockSpec((None, None, 2, D_MODEL, chunk), lambda b, c: (layer, b, 0, 0, c))],
        out_specs=pl.BlockSpec((None, 1, D_MODEL), row),
        out_shape=jax.ShapeDtypeStruct((DEC_BATCH, 1, D_MODEL), f32),
        scratch_shapes=[pltpu.VMEM((ng * D_MODEL, LANES), f32),
                        pltpu.VMEM((ng * N_HEADS, LANES), f32), pltpu.VMEM((ng * N_HEADS, LANES), f32),
                        pltpu.VMEM((ng * D_MODEL, LANES), f32), pltpu.VMEM((D_MODEL, LANES), f32),
                        pltpu.VMEM((3, N_HEADS, LANES), f32)],
        compiler_params=_cp(("parallel", "arbitrary")),
        name="dil_decode",
    )(q, kn, vn, c1, c2, c3)


def kernel(x_prompt, x_sample, cache_fox_k, cache_fox_v, cache_fox_logf, cache_dil_w128, cache_dil_w512,
           cache_dil_w2048, page_table, fox_w_qkv, fox_w_f, fox_b_f, fox_w_o, dil_w_qkv, dil_w_o,
           moe_w_router, moe_b_router, moe_w1, moe_b1, moe_w2, moe_b2, ln_g, ln_b):
    n_phys = cache_fox_k.shape[1]
    pad_lanes = LANES - N_HEADS
    ck = cache_fox_k.transpose(0, 1, 3, 4, 2).reshape(-1, n_phys, D_MODEL, PAGE_SIZE)
    cv = cache_fox_v.transpose(0, 1, 3, 4, 2).reshape(-1, n_phys, D_MODEL, PAGE_SIZE)
    cl = cache_fox_logf.transpose(0, 1, 3, 2)
    dil_caches = [c.transpose(0, 1, 3, 4, 5, 2).reshape(c.shape[0], DEC_BATCH, 2, D_MODEL, c.shape[2])
                  for c in (cache_dil_w128, cache_dil_w512, cache_dil_w2048)]
    cos_p, sin_p = _rope_tables(jnp.arange(SEQ))
    cos_s, sin_s = _rope_tables(jnp.full((1,), PAST_LEN))

    x = jnp.concatenate([x_prompt.reshape(N_PROMPT, D_MODEL), x_sample.reshape(DEC_BATCH, D_MODEL),
                         jnp.zeros((NT - N_PROMPT - DEC_BATCH, D_MODEL), f32)], axis=0)

    fk_p, fk_s, fv_p, fv_s, fl_p, fl_s = [], [], [], [], [], []
    dil_p = [[] for _ in DIL_CONFIGS]
    dil_s = [[] for _ in DIL_CONFIGS]
    for i in range(DEPTH):
        j = i // 2
        if i % 2 == 0:
            w = fox_w_qkv[j].astype(bf16)
            wf = jnp.pad(fox_w_f[j], ((0, 0), (0, pad_lanes)))
            bfv = jnp.pad(fox_b_f[j], (0, pad_lanes))[None, :]
            q, ktb, vb, kt, vt, lft, ct, ccol = fox_proj_prompt(x, w, wf, bfv)
            o_main = fox_attn_prompt(q, ktb, vb, ccol, ct)
            qs, ks, vs, lfs = fox_proj_sample(x, w, wf, bfv)
            o_s = fox_decode(page_table, qs[:, None], ks[:, None], vs[:, None], lfs[:, None], ck, cv, cl, j)
            fk_p.append(kt.reshape(BATCH, N_HEADS, HEAD_DIM, SEQ).transpose(0, 3, 1, 2))
            fv_p.append(vt.reshape(BATCH, N_HEADS, HEAD_DIM, SEQ).transpose(0, 3, 1, 2))
            fl_p.append(lft.transpose(0, 2, 1))
            fk_s.append(ks.reshape(DEC_BATCH, 1, N_HEADS, HEAD_DIM))
            fv_s.append(vs.reshape(DEC_BATCH, 1, N_HEADS, HEAD_DIM))
            fl_s.append(lfs[:, :N_HEADS].reshape(DEC_BATCH, 1, N_HEADS))
            wo = fox_w_o[j].astype(bf16)
        else:
            w = dil_w_qkv[j].astype(bf16)
            os_, ls_ = [], []
            for gi, (win, _) in enumerate(DIL_CONFIGS):
                qr, kr, vr, kvt = dil_proj_prompt(x, w, cos_p, sin_p, gi)
                o_g, l_g = dil_attn_prompt(qr, kr, vr)
                os_.append(o_g)
                ls_.append(l_g)
                keep = min(win, SEQ)
                rows = kvt[:, :, :, SEQ - keep:].reshape(BATCH, 2, N_HEADS, HEAD_DIM, keep)
                dil_p[gi].append(rows.transpose(0, 4, 1, 2, 3))
            o_main = dil_merge_prompt(*os_, *ls_)
            qs, ks, vs = dil_proj_sample(x, w, cos_s, sin_s)
            o_s = dil_decode(qs.transpose(1, 0, 2), ks.transpose(1, 0, 2), vs.transpose(1, 0, 2),
                             *dil_caches, j)
            for gi in range(len(DIL_CONFIGS)):
                dil_s[gi].append(jnp.stack([ks[gi], vs[gi]], axis=1).reshape(DEC_BATCH, 1, 2, N_HEADS, HEAD_DIM))
            wo = dil_w_o[j].astype(bf16)

        o_tail = jnp.pad(o_s.reshape(DEC_BATCH, D_MODEL).astype(bf16), ((0, ROW_TILE - DEC_BATCH), (0, 0)))
        wr = jnp.pad(moe_w_router[i], ((0, 0), (0, LANES - N_EXPERTS)))
        br = jnp.concatenate([moe_b_router[i], jnp.full((LANES - N_EXPERTS,), NEG_INF, f32)])[None, :]
        x1, x1t, idx, gate, rank, cnt = post_attn(o_main, o_tail, x, wo, ln_g[i, 0][None, :], ln_b[i, 0][None, :], wr, br)
        ge, nv, nu, idx_arr = _route(idx, rank, cnt)
        y = moe_experts(ge, nv, nu, x1t, idx_arr, moe_w1, moe_b1[:, :, None, :], moe_w2, moe_b2[:, :, None, :], i)
        x = moe_combine(y, gate, x1, ln_g[i, 1][None, :], ln_b[i, 1][None, :])

    xp = x[:N_PROMPT].reshape(BATCH, SEQ, D_MODEL)
    xs = x[N_PROMPT:N_PROMPT + DEC_BATCH].reshape(DEC_BATCH, 1, D_MODEL)
    return (xp, xs,
            jnp.stack(fk_p), jnp.stack(fk_s), jnp.stack(fv_p), jnp.stack(fv_s), jnp.stack(fl_p), jnp.stack(fl_s),
            jnp.stack(dil_p[0]), jnp.stack(dil_s[0]), jnp.stack(dil_p[1]), jnp.stack(dil_s[1]),
            jnp.stack(dil_p[2]), jnp.stack(dil_s[2]))
```
